```python
import math
import jax, jax.numpy as jnp
from jax import lax
import numpy as np

D_MODEL = 1024
BATCH = 8
SEQ = 2048
DEPTH = 4

GRID_W = 64
CTX_LEN = 256
MIX_WIDTH = D_MODEL
NA_HEADS = 6
NA_HEAD_DIM = 64
NA_WIDTH = NA_HEADS * NA_HEAD_DIM
NA_WIN_H = 8
NA_WIN_W = 16
NA_QBLOCK = 16
NA_STRIP = NA_QBLOCK + NA_WIN_W
DIFF_HEADS = 4
DIFF_QK_DIM = 48
DIFF_V_DIM = 2 * DIFF_QK_DIM
DIFF_WIDTH = DIFF_HEADS * DIFF_V_DIM
DIFF_QBLOCK = 128
ROPE_BASE = 10000.0
CONV_WIDTH = MIX_WIDTH - NA_WIDTH - DIFF_WIDTH
CONV_TAPS = 31
COL_SPLITS = (NA_WIDTH, NA_WIDTH, DIFF_WIDTH, DIFF_WIDTH,
              NA_WIDTH, NA_WIDTH, DIFF_WIDTH, DIFF_WIDTH,
              CONV_WIDTH, CONV_WIDTH, CONV_WIDTH)
IN_COLS = 4 * NA_WIDTH + 4 * DIFF_WIDTH + 3 * CONV_WIDTH
EPS = 1e-6
NEG_INF = -1e30

kernel_name = 'hybrid_na_diffattn_conformer_dit_trunk'


def _rms(x, g):
    xf = x.astype(jnp.float32)
    y = xf * lax.rsqrt(jnp.mean(xf * xf, axis=-1, keepdims=True) + EPS)
    return (y * g.astype(jnp.float32)).astype(x.dtype)


def _layer_norm(x, g, b):
    xf = x.astype(jnp.float32)
    mu = jnp.mean(xf, axis=-1, keepdims=True)
    var = jnp.mean(jnp.square(xf - mu), axis=-1, keepdims=True)
    y = (xf - mu) * lax.rsqrt(var + EPS) * g.astype(jnp.float32) + b.astype(jnp.float32)
    return y.astype(x.dtype)


def _heads(z, n_heads):
    return z.reshape(z.shape[0], z.shape[1], n_heads, z.shape[-1] // n_heads)


def _split_cols(z, n):
    idx = np.cumsum(COL_SPLITS)[:n - 1].tolist()
    return jnp.split(z, idx, axis=-1)


def _rope_1d(x, pos):
    d = x.shape[-1]
    inv = ROPE_BASE ** (-jnp.arange(0, d, 2, dtype=jnp.float32) / d)
    ang = pos.astype(jnp.float32)[:, None] * inv[None, :]
    cos, sin = jnp.cos(ang)[:, None, :], jnp.sin(ang)[:, None, :]
    xf = x.astype(jnp.float32)
    x1, x2 = xf[..., :d // 2], xf[..., d // 2:]
    return jnp.concatenate([x1 * cos - x2 * sin, x2 * cos + x1 * sin], axis=-1).astype(x.dtype)


def _rope_2d(x, rows, cols):
    half = x.shape[-1] // 2
    return jnp.concatenate([_rope_1d(x[..., :half], rows), _rope_1d(x[..., half:], cols)], axis=-1)


def _na_tables():
    ncb = GRID_W // NA_QBLOCK
    qcols = np.arange(GRID_W).reshape(ncb, NA_QBLOCK)
    kstart = np.clip(np.arange(ncb) * NA_QBLOCK - NA_WIN_W // 2, 0, GRID_W - NA_STRIP)
    kcols = kstart[:, None] + np.arange(NA_STRIP)
    cstart = np.clip(qcols - NA_WIN_W // 2, 0, GRID_W - NA_WIN_W)
    kc = kcols[:, None, :]
    valid = (kc >= cstart[:, :, None]) & (kc < cstart[:, :, None] + NA_WIN_W)
    dc_idx = np.clip(kc - qcols[:, :, None] + NA_WIN_W - 1, 0, 2 * NA_WIN_W - 2)
    return kcols, valid, dc_idx


def _neighbourhood_attention(q, k, v, kc, vc, rpb):
    B, T, H, dh = q.shape
    n_rows = T // GRID_W
    wr = min(NA_WIN_H, n_rows)
    ncb = GRID_W // NA_QBLOCK
    nk = wr * NA_STRIP
    kcols, valid, dc_idx = _na_tables()
    mask = np.broadcast_to(valid[:, :, None, :], (ncb, NA_QBLOCK, wr, NA_STRIP)).reshape(ncb, NA_QBLOCK, nk)
    qg = (q * NA_HEAD_DIM ** -0.5).reshape(B, n_rows, GRID_W, H, dh)
    kg = k.reshape(B, n_rows, GRID_W, H, dh)
    vg = v.reshape(B, n_rows, GRID_W, H, dh)

    def row(r):
        rs = jnp.clip(r - wr // 2, 0, n_rows - wr)

        def strips(t):
            t = lax.dynamic_slice_in_dim(t, rs, wr, axis=1)[:, :, kcols]
            return t.transpose(0, 2, 1, 3, 4, 5).reshape(B, ncb, nk, H, dh)

        ks_, vs_ = strips(kg), strips(vg)
        qr = lax.dynamic_index_in_dim(qg, r, axis=1, keepdims=False).reshape(B, ncb, NA_QBLOCK, H, dh)
        dr_idx = rs + jnp.arange(wr) - r + (NA_WIN_H - 1)
        bias = rpb[:, dr_idx][:, :, dc_idx].transpose(0, 2, 3, 1, 4).reshape(H, ncb, NA_QBLOCK, nk)
        s_loc = jnp.einsum('bnqhd,bnkhd->bhnqk', qr, ks_).astype(jnp.float32) + bias.astype(jnp.float32)
        s_loc = jnp.where(mask, s_loc, NEG_INF)
        s_ctx = jnp.einsum('bnqhd,bkhd->bhnqk', qr, kc).astype(jnp.float32)
        p = jax.nn.softmax(jnp.concatenate([s_loc, s_ctx], axis=-1), axis=-1).astype(v.dtype)
        o = (jnp.einsum('bhnqk,bnkhd->bnqhd', p[..., :nk], vs_)
             + jnp.einsum('bhnqk,bkhd->bnqhd', p[..., nk:], vc))
        return o.reshape(B, GRID_W, H, dh)

    out = lax.map(row, jnp.arange(n_rows))
    return out.transpose(1, 0, 2, 3, 4).reshape(B, T, H * dh)


def _dense_attend(q, k, v):
    s = jnp.einsum('bqhd,bkhd->bhqk', q, k).astype(jnp.float32) * (q.shape[-1] ** -0.5)
    p = jax.nn.softmax(s, axis=-1).astype(v.dtype)
    return jnp.einsum('bhqk,bkhd->bqhd', p, v)


def _diff_attend(q1, q2, k1, k2, v, lam):
    B, Tq, H, d = q1.shape
    nb = Tq // DIFF_QBLOCK
    scale = d ** -0.5

    def to_blocks(t):
        return t.reshape(B, nb, DIFF_QBLOCK, H, d).swapaxes(0, 1)

    def blk(qs):
        a, b = qs
        s1 = jnp.einsum('bqhd,bkhd->bhqk', a, k1).astype(jnp.float32) * scale
        s2 = jnp.einsum('bqhd,bkhd->bhqk', b, k2).astype(jnp.float32) * scale
        w = jax.nn.softmax(s1, axis=-1) - lam * jax.nn.softmax(s2, axis=-1)
        return jnp.einsum('bhqk,bkhe->bqhe', w.astype(v.dtype), v)

    o = lax.map(blk, (to_blocks(q1), to_blocks(q2)))
    return o.swapaxes(0, 1).reshape(B, Tq, H, v.shape[-1])


def _diff_merge(o, subln_g, lam_init):
    return (_rms(o, subln_g) * (1.0 - lam_init)).reshape(o.shape[0], o.shape[1], -1)


def _conformer_conv(a, b, conv_w, conv_b, ln_g, ln_b):
    u = a * jax.nn.sigmoid(b)
    pad = CONV_TAPS // 2
    y = lax.conv_general_dilated(u, conv_w[:, None, :].astype(u.dtype), (1,), [(pad, pad)],
                                 dimension_numbers=('NWC', 'WIO', 'NWC'),
                                 feature_group_count=CONV_WIDTH) + conv_b
    return jax.nn.silu(_layer_norm(y, ln_g, ln_b))


def _layer(x, xc, c, c_ctx, w_mod, b_mod, g_pre, g_post, w_in, w_out, rpb,
           lq1, lk1, lq2, lk2, subln_g, conv_w, conv_b, ln_g, ln_b, layer_idx, update_ctx):
    B, T, _ = x.shape
    t = jnp.arange(T)
    rows, cols = t // GRID_W, t % GRID_W
    d = DIFF_QK_DIM

    shift, scale, gate = jnp.split((jax.nn.silu(c) @ w_mod + b_mod)[:, None, :], 3, axis=-1)
    n_mod = 3 if update_ctx else 2
    mod_c = jnp.split(jax.nn.silu(c_ctx) @ w_mod[:, :n_mod * D_MODEL] + b_mod[:n_mod * D_MODEL], n_mod)

    h = _rms(x, g_pre) * (1.0 + scale) + shift
    hc = _rms(xc, g_pre) * (1.0 + mod_c[1]) + mod_c[0]

    na_k, na_v, df_k, df_v, na_q, na_g, df_q, df_g, cv_a, cv_b, cv_g = _split_cols(h @ w_in, len(COL_SPLITS))
    n_cols = len(COL_SPLITS) if update_ctx else 4
    zc = _split_cols(hc @ w_in[:, :sum(COL_SPLITS[:n_cols])], n_cols)
    nak_c, nav_c, dfk_c, dfv_c = zc[:4]

    nak_c, nav_c = _heads(nak_c, NA_HEADS), _heads(nav_c, NA_HEADS)
    a_out = _neighbourhood_attention(_heads(na_q, NA_HEADS), _heads(na_k, NA_HEADS), _heads(na_v, NA_HEADS),
                                     nak_c, nav_c, rpb)

    lam_init = 0.8 - 0.6 * math.exp(-0.3 * layer_idx)
    lam = (jnp.exp(jnp.sum(lq1 * lk1).astype(jnp.float32))
           - jnp.exp(jnp.sum(lq2 * lk2).astype(jnp.float32)) + lam_init)
    rope = lambda z: _rope_2d(z, rows, cols)
    q, k = _heads(df_q, DIFF_HEADS), _heads(df_k, DIFF_HEADS)
    kc, vc = _heads(dfk_c, DIFF_HEADS), _heads(dfv_c, DIFF_HEADS)
    k1 = jnp.concatenate([rope(k[..., :d]), kc[..., :d]], axis=1)
    k2 = jnp.concatenate([rope(k[..., d:]), kc[..., d:]], axis=1)
    vd = jnp.concatenate([_heads(df_v, DIFF_HEADS), vc], axis=1)
    b_out = _diff_merge(_diff_attend(rope(q[..., :d]), rope(q[..., d:]), k1, k2, vd, lam), subln_g, lam_init)

    c_out = _conformer_conv(cv_a, cv_b, conv_w, conv_b, ln_g, ln_b)

    y = jnp.concatenate([a_out * jax.nn.silu(na_g), b_out * jax.nn.silu(df_g),
                         c_out * jax.nn.silu(cv_g)], axis=-1) @ w_out
    x_new = x + gate * _rms(y, g_post)

    if update_ctx:
        naq_c, nag_c, dfq_c, dfg_c, cva_c, cvb_c, cvg_c = zc[4:]
        a_c = _dense_attend(_heads(naq_c, NA_HEADS), nak_c, nav_c).reshape(B, -1, NA_WIDTH)
        qc = _heads(dfq_c, DIFF_HEADS)
        b_c = _diff_merge(_diff_attend(qc[..., :d], qc[..., d:], kc[..., :d], kc[..., d:], vc, lam),
                          subln_g, lam_init)
        c_c = _conformer_conv(cva_c, cvb_c, conv_w, conv_b, ln_g, ln_b)
        yc = jnp.concatenate([a_c * jax.nn.silu(nag_c), b_c * jax.nn.silu(dfg_c),
                              c_c * jax.nn.silu(cvg_c)], axis=-1) @ w_out
        xc = xc + mod_c[2] * _rms(yc, g_post)
    return x_new, xc


def setup_inputs(seed: int = 0) -> dict:
    key = jax.random.key(seed)
    ks = jax.random.split(key, 20)
    f32 = jnp.float32

    def nrm(k, shape, s):
        return s * jax.random.normal(k, shape, f32)

    return {
        'x': nrm(ks[0], (BATCH, SEQ, D_MODEL), 1.0),
        'c': nrm(ks[1], (BATCH, D_MODEL), 1.0),
        'ctx': nrm(ks[2], (BATCH, CTX_LEN, D_MODEL), 1.0),
        'c_ctx': nrm(ks[3], (D_MODEL,), 1.0),
        'w_mod': nrm(ks[4], (DEPTH, D_MODEL, 3 * D_MODEL), 0.5 * D_MODEL ** -0.5),
        'b_mod': nrm(ks[5], (DEPTH, 3 * D_MODEL), 0.02),
        'g_pre': 1.0 + nrm(ks[6], (DEPTH, D_MODEL), 0.1),
        'g_post': 1.0 + nrm(ks[7], (DEPTH, D_MODEL), 0.1),
        'w_in': nrm(ks[8], (DEPTH, D_MODEL, IN_COLS), D_MODEL ** -0.5),
        'w_out': nrm(ks[9], (DEPTH, MIX_WIDTH, D_MODEL), MIX_WIDTH ** -0.5),
        'na_rpb': nrm(ks[10], (DEPTH, NA_HEADS, 2 * NA_WIN_H - 1, 2 * NA_WIN_W - 1), 0.1),
        'diff_lq1': nrm(ks[11], (DEPTH, DIFF_QK_DIM), 0.1),
        'diff_lk1': nrm(ks[12], (DEPTH, DIFF_QK_DIM), 0.1),
        'diff_lq2': nrm(ks[13], (DEPTH, DIFF_QK_DIM), 0.1),
        'diff_lk2': nrm(ks[14], (DEPTH, DIFF_QK_DIM), 0.1),
        'diff_subln_g': 1.0 + nrm(ks[15], (DEPTH, DIFF_V_DIM), 0.1),
        'conv_w': nrm(ks[16], (DEPTH, CONV_TAPS, CONV_WIDTH), CONV_TAPS ** -0.5),
        'conv_b': nrm(ks[17], (DEPTH, CONV_WIDTH), 0.02),
        'conv_ln_g': 1.0 + nrm(ks[18], (DEPTH, CONV_WIDTH), 0.1),
        'conv_ln_b': nrm(ks[19], (DEPTH, CONV_WIDTH), 0.02),
    }


def reference(x, c, ctx, c_ctx, w_mod, b_mod, g_pre, g_post, w_in, w_out, na_rpb,
              diff_lq1, diff_lk1, diff_lq2, diff_lk2, diff_subln_g,
              conv_w, conv_b, conv_ln_g, conv_ln_b):
    xc = ctx
    for l in range(DEPTH):
        x, xc = _layer(x, xc, c, c_ctx, w_mod[l], b_mod[l], g_pre[l], g_post[l], w_in[l], w_out[l],
                       na_rpb[l], diff_lq1[l], diff_lk1[l], diff_lq2[l], diff_lk2[l], diff_subln_g[l],
                       conv_w[l], conv_b[l], conv_ln_g[l], conv_ln_b[l],
                       layer_idx=l, update_ctx=(l < DEPTH - 1))
    return x
```

```python
import functools
import math

import numpy as np
import jax
import jax.numpy as jnp
from jax import lax
from jax.experimental import pallas as pl
from jax.experimental.pallas import tpu as pltpu

F32 = jnp.float32
BF16 = jnp.bfloat16

D_MODEL = 1024
DEPTH = 4
GRID_W = 64
NA_HEADS = 6
NA_HEAD_DIM = 64
NA_WIDTH = NA_HEADS * NA_HEAD_DIM
NA_WIN_H = 8
NA_WIN_W = 16
DIFF_HEADS = 4
DIFF_QK_DIM = 48
DIFF_V_DIM = 2 * DIFF_QK_DIM
DIFF_WIDTH = DIFF_HEADS * DIFF_V_DIM
CONV_WIDTH = D_MODEL - NA_WIDTH - DIFF_WIDTH
CONV_TAPS = 31
ROPE_BASE = 10000.0
EPS = 1e-6
NEG_INF = -1e30

LANES = 128
NA_PAIRS = NA_HEADS // 2
KEY_BLK = 256
NA_QBLK = 256
NA_ROWS_PER_QBLK = NA_QBLK // GRID_W
NA_LOC_BLKS = 3
NA_PATTERNS = 3
VMEM_LIMIT = 56 * 1024 * 1024

C_KN = 0
C_KD = C_KN + NA_WIDTH
C_QN = C_KD + DIFF_HEADS * LANES
C_QD = C_QN + NA_WIDTH
C_VN = C_QD + DIFF_HEADS * LANES
C_VD = C_VN + NA_WIDTH
C_GATE = C_VD + DIFF_WIDTH
C_CA = C_GATE + D_MODEL
C_CB = C_CA + CONV_WIDTH
IN_COLS_PADDED = C_CB + CONV_WIDTH


def _cparams(n_axes, vmem=VMEM_LIMIT):
    return pltpu.CompilerParams(dimension_semantics=("arbitrary",) * n_axes, vmem_limit_bytes=vmem)


def _silu(z):
    return z * jax.nn.sigmoid(z)


def _dot(a, b):
    return jnp.dot(a, b, preferred_element_type=F32)


def _dot_nt(a, b):
    return lax.dot_general(a, b, (((1,), (1,)), ((), ())), preferred_element_type=F32)


def _mod_kernel(cc_ref, w_ref, b_ref, o_ref):
    s = _silu(cc_ref[...])
    o_ref[0] = _dot(s, w_ref[0]) + b_ref[0]


def _modulation(cc, w_mod, b_mod):
    n_rows = cc.shape[0]
    return pl.pallas_call(
        _mod_kernel,
        grid=(DEPTH, 3),
        in_specs=[
            pl.BlockSpec((n_rows, D_MODEL), lambda l, j: (0, 0)),
            pl.BlockSpec((1, D_MODEL, D_MODEL), lambda l, j: (l, 0, j)),
            pl.BlockSpec((1, 1, D_MODEL), lambda l, j: (l, 0, j)),
        ],
        out_specs=pl.BlockSpec((1, n_rows, D_MODEL), lambda l, j: (l, 0, j)),
        out_shape=jax.ShapeDtypeStruct((DEPTH, n_rows, 3 * D_MODEL), F32),
        compiler_params=_cparams(2),
        name="adaln_mod",
    )(cc, w_mod, b_mod.reshape(DEPTH, 1, 3 * D_MODEL))


def _na_bias_kernel(rpb_ref, o_ref):
    l = pl.program_id(0)
    h = pl.program_id(1)
    n_dr = 2 * NA_WIN_H - 1
    n_dc = 2 * NA_WIN_W - 1
    base = (l * NA_HEADS + h) * (n_dr * n_dc)
    kc = lax.broadcasted_iota(jnp.int32, (GRID_W, LANES), 0)
    lane = lax.broadcasted_iota(jnp.int32, (GRID_W, LANES), 1)
    qc = lane % GRID_W
    cstart = jnp.clip(qc - NA_WIN_W // 2, 0, GRID_W - NA_WIN_W)
    col_ok = (kc >= cstart) & (kc < cstart + NA_WIN_W)
    dc = kc - qc + (NA_WIN_W - 1)
    neg = jnp.full((GRID_W, LANES), NEG_INF, F32)
    dc_masks = [col_ok & (dc == i) for i in range(n_dc)]
    tiles = []
    for dr in range(n_dr):
        t = neg
        for i in range(n_dc):
            t = jnp.where(dc_masks[i], rpb_ref[base + dr * n_dc + i], t)
        tiles.append(t)
    left = lane < GRID_W

    def tile(dr, ok):
        return tiles[dr] if ok else neg

    patterns = (
        (lambda i: (0, NA_WIN_H), NA_WIN_H - 1),
        (lambda i: (i, i + NA_WIN_H), NA_WIN_H - 1 - NA_ROWS_PER_QBLK),
        (lambda i: (4, 4 + NA_WIN_H), -1),
    )
    for p, (rng, off) in enumerate(patterns):
        for j in range(NA_LOC_BLKS * KEY_BLK // GRID_W):
            halves = []
            for i0 in (0, 2):
                pair = []
                for i in (i0, i0 + 1):
                    lo, hi = rng(i)
                    pair.append(tile(j - i + off, lo <= j < hi) if 0 <= j - i + off < n_dr else neg)
                halves.append(jnp.where(left, pair[0], pair[1]))
            o_ref[0, 0, p, j * GRID_W:(j + 1) * GRID_W, :] = jnp.concatenate(halves, axis=1)


def _na_bias_tables(rpb_flat):
    n_keys = NA_LOC_BLKS * KEY_BLK
    return pl.pallas_call(
        _na_bias_kernel,
        grid=(DEPTH, NA_HEADS),
        in_specs=[pl.BlockSpec(memory_space=pltpu.SMEM)],
        out_specs=pl.BlockSpec((1, 1, NA_PATTERNS, n_keys, NA_QBLK), lambda l, h: (l, h, 0, 0, 0)),
        out_shape=jax.ShapeDtypeStruct((DEPTH, NA_HEADS, NA_PATTERNS, n_keys, NA_QBLK), F32),
        compiler_params=_cparams(2),
        name="na_bias_tables",
    )(rpb_flat)


def _rope_tables(seq):
    half = DIFF_QK_DIM // 2
    t = np.arange(seq)
    pos = np.stack([t // GRID_W, t % GRID_W], axis=1).astype(np.float64)
    lane = np.arange(LANES)
    live = lane < DIFF_V_DIM
    seg = np.where(live, lane // half, 0)
    i = lane % half
    inv = ROPE_BASE ** (-(2.0 * (i % (half // 2))) / half)
    ang = pos[:, seg % 2] * inv[None, :]
    cos = np.where(live[None, :], np.cos(ang), 1.0)
    sgn = np.where(i < half // 2, -1.0, 1.0)
    sin = np.where(live[None, :], np.sin(ang) * sgn[None, :], 0.0)
    return jnp.asarray(cos, F32), jnp.asarray(sin, F32)


def _inproj_kernel(*refs, rope):
    if rope:
        x_ref, ss_ref, gpre_ref, w_ref, cos_ref, sin_ref = refs[:6]
        outs = refs[6:]
    else:
        x_ref, ss_ref, gpre_ref, w_ref = refs[:4]
        outs = refs[4:]
    kn_ref, kd_ref, qn_ref, qd_ref, vnt_ref, vdt_ref, gate_ref, u_ref = outs

    x = x_ref[0]
    ms = jnp.mean(x * x, axis=-1, keepdims=True)
    hn = x * lax.rsqrt(ms + EPS) * gpre_ref[...]
    ss = ss_ref[0, 0]
    hb = (hn * (1.0 + ss[:, D_MODEL:]) + ss[:, :D_MODEL]).astype(BF16)

    def proj(a, b):
        return _dot(hb, w_ref[:, a:b])

    if rope:
        cos = cos_ref[...]
        sin = sin_ref[...]
        lane = lax.broadcasted_iota(jnp.int32, cos.shape, 1)
        quarter = DIFF_QK_DIM // 4
        first = (lane % (2 * quarter)) < quarter

        def rot(zb):
            up = pltpu.roll(zb, LANES - quarter, axis=1)
            dn = pltpu.roll(zb, quarter, axis=1)
            return zb * cos + jnp.where(first, up, dn) * sin
    else:
        def rot(zb):
            return zb

    z = proj(C_KN, C_KD)
    for p in range(NA_PAIRS):
        kn_ref[0, p] = z[:, p * LANES:(p + 1) * LANES].astype(BF16)
    z = proj(C_KD, C_QN)
    for h in range(DIFF_HEADS):
        kd_ref[0, h] = rot(z[:, h * LANES:(h + 1) * LANES]).astype(BF16)
    z = proj(C_QN, C_QD) * (NA_HEAD_DIM ** -0.5)
    for p in range(NA_PAIRS):
        qn_ref[0, p] = z[:, p * LANES:(p + 1) * LANES].astype(BF16)
    z = proj(C_QD, C_VN)
    for h in range(DIFF_HEADS):
        qd_ref[0, h] = (rot(z[:, h * LANES:(h + 1) * LANES]) * (DIFF_QK_DIM ** -0.5)).astype(BF16)
    zt = proj(C_VN, C_VD).T
    for p in range(NA_PAIRS):
        vnt_ref[0, p] = zt[p * LANES:(p + 1) * LANES].astype(BF16)
    zt = proj(C_VD, C_GATE).T
    for h in range(DIFF_HEADS):
        vdt_ref[0, h] = zt[h * DIFF_V_DIM:(h + 1) * DIFF_V_DIM].astype(BF16)
    gate_ref[0] = _silu(proj(C_GATE, C_CA))
    u_ref[0] = proj(C_CA, C_CB) * jax.nn.sigmoid(proj(C_CB, IN_COLS_PADDED))


def _inproj(x, ss, g_pre, w, tables, tm):
    b, s, _ = x.shape
    rope = tables is not None
    ss_idx = (lambda bi, i: (bi, 0, 0, 0)) if ss.shape[0] == b else (lambda bi, i: (0, 0, 0, 0))
    in_specs = [
        pl.BlockSpec((1, tm, D_MODEL), lambda bi, i: (bi, i, 0)),
        pl.BlockSpec((1, 1, 1, 2 * D_MODEL), ss_idx),
        pl.BlockSpec((1, D_MODEL), lambda bi, i: (0, 0)),
        pl.BlockSpec((D_MODEL, IN_COLS_PADDED), lambda bi, i: (0, 0)),
    ]
    args = [x, ss, g_pre.reshape(1, D_MODEL), w]
    if rope:
        in_specs += [pl.BlockSpec((tm, LANES), lambda bi, i: (i, 0))] * 2
        args += list(tables)
    row_major = lambda n: pl.BlockSpec((1, n, tm, LANES), lambda bi, i: (bi, 0, i, 0))
    out_specs = [
        row_major(NA_PAIRS), row_major(DIFF_HEADS), row_major(NA_PAIRS), row_major(DIFF_HEADS),
        pl.BlockSpec((1, NA_PAIRS, LANES, tm), lambda bi, i: (bi, 0, 0, i)),
        pl.BlockSpec((1, DIFF_HEADS, DIFF_V_DIM, tm), lambda bi, i: (bi, 0, 0, i)),
        pl.BlockSpec((1, tm, D_MODEL), lambda bi, i: (bi, i, 0)),
        pl.BlockSpec((1, tm, CONV_WIDTH), lambda bi, i: (bi, i, 0)),
    ]
    out_shape = [
        jax.ShapeDtypeStruct((b, NA_PAIRS, s, LANES), BF16),
        jax.ShapeDtypeStruct((b, DIFF_HEADS, s, LANES), BF16),
        jax.ShapeDtypeStruct((b, NA_PAIRS, s, LANES), BF16),
        jax.ShapeDtypeStruct((b, DIFF_HEADS, s, LANES), BF16),
        jax.ShapeDtypeStruct((b, NA_PAIRS, LANES, s), BF16),
        jax.ShapeDtypeStruct((b, DIFF_HEADS, DIFF_V_DIM, s), BF16),
        jax.ShapeDtypeStruct((b, s, D_MODEL), F32),
        jax.ShapeDtypeStruct((b, s, CONV_WIDTH), F32),
    ]
    return pl.pallas_call(
        functools.partial(_inproj_kernel, rope=rope),
        grid=(b, s // tm),
        in_specs=in_specs,
        out_specs=out_specs,
        out_shape=out_shape,
        compiler_params=_cparams(2),
        name="inproj_lat" if rope else "inproj_ctx",
    )(*args)


def _na_heads(q_ref, k_refs, vt_refs, bias_ref, o_ref):
    lane = lax.broadcasted_iota(jnp.int32, (NA_QBLK, LANES), 1)
    for p in range(NA_PAIRS):
        q = q_ref[0, p]
        for hh in range(2):
            head = 2 * p + hh
            in_head = (lane >= hh * NA_HEAD_DIM) & (lane < (hh + 1) * NA_HEAD_DIM)
            qm = jnp.where(in_head, q, jnp.zeros_like(q))
            s = []
            for i, k_ref in enumerate(k_refs):
                si = _dot_nt(k_ref[0, p], qm)
                if i > 0:
                    si = si + bias_ref[head, 0, (i - 1) * KEY_BLK:i * KEY_BLK, :]
                s.append(si)
            m = functools.reduce(jnp.maximum, [jnp.max(si, axis=0, keepdims=True) for si in s])
            e = [jnp.exp(si - m) for si in s]
            denom = functools.reduce(jnp.add, [jnp.sum(ei, axis=0, keepdims=True) for ei in e])
            rows = slice(hh * NA_HEAD_DIM, (hh + 1) * NA_HEAD_DIM)
            o = functools.reduce(jnp.add, [
                _dot(vt_ref[0, p, rows, :], ei.astype(BF16))
                for vt_ref, ei in zip(vt_refs, e)])
            o_ref[0, p, rows, :] = o * (1.0 / denom)


def _na_lat_kernel(q_ref, kc_ref, k0_ref, k1_ref, k2_ref, vc_ref, v0_ref, v1_ref, v2_ref, bias_ref, o_ref):
    _na_heads(q_ref, (kc_ref, k0_ref, k1_ref, k2_ref), (vc_ref, v0_ref, v1_ref, v2_ref), bias_ref, o_ref)


def _na_ctx_kernel(q_ref, kc_ref, vc_ref, o_ref):
    _na_heads(q_ref, (kc_ref,), (vc_ref,), None, o_ref)


def _na_lat(qn, kn_c, kn, vnt_c, vnt, bias):
    b, _, t, _ = qn.shape
    n_qblk = t // NA_QBLK
    last_start = t // KEY_BLK - NA_LOC_BLKS

    def start(j):
        return jnp.clip(j - 1, 0, last_start)

    def pattern(j):
        return jnp.where(j == 0, 0, jnp.where(j == n_qblk - 1, 2, 1))

    row_blk = (1, NA_PAIRS, KEY_BLK, LANES)
    col_blk = (1, NA_PAIRS, LANES, KEY_BLK)
    k_specs = [pl.BlockSpec(row_blk, functools.partial(lambda bi, j, i: (bi, 0, start(j) + i, 0), i=i))
               for i in range(NA_LOC_BLKS)]
    v_specs = [pl.BlockSpec(col_blk, functools.partial(lambda bi, j, i: (bi, 0, 0, start(j) + i), i=i))
               for i in range(NA_LOC_BLKS)]
    return pl.pallas_call(
        _na_lat_kernel,
        grid=(b, n_qblk),
        in_specs=[pl.BlockSpec(row_blk, lambda bi, j: (bi, 0, j, 0)),
                  pl.BlockSpec(row_blk, lambda bi, j: (bi, 0, 0, 0))] + k_specs
                 + [pl.BlockSpec(col_blk, lambda bi, j: (bi, 0, 0, 0))] + v_specs
                 + [pl.BlockSpec((NA_HEADS, 1, NA_LOC_BLKS * KEY_BLK, NA_QBLK),
                                 lambda bi, j: (0, pattern(j), 0, 0))],
        out_specs=pl.BlockSpec(col_blk, lambda bi, j: (bi, 0, 0, j)),
        out_shape=jax.ShapeDtypeStruct((b, NA_PAIRS, LANES, t), F32),
        compiler_params=_cparams(2),
        name="na_lat",
    )(qn, kn_c, kn, kn, kn, vnt_c, vnt, vnt, vnt, bias)


def _na_ctx(qn_c, kn_c, vnt_c):
    b, _, c, _ = qn_c.shape
    row_blk = (1, NA_PAIRS, c, LANES)
    col_blk = (1, NA_PAIRS, LANES, c)
    return pl.pallas_call(
        _na_ctx_kernel,
        grid=(b,),
        in_specs=[pl.BlockSpec(row_blk, lambda bi: (bi, 0, 0, 0)),
                  pl.BlockSpec(row_blk, lambda bi: (bi, 0, 0, 0)),
                  pl.BlockSpec(col_blk, lambda bi: (bi, 0, 0, 0))],
        out_specs=pl.BlockSpec(col_blk, lambda bi: (bi, 0, 0, 0)),
        out_shape=jax.ShapeDtypeStruct((b, NA_PAIRS, LANES, c), F32),
        compiler_params=_cparams(1),
        name="na_ctx",
    )(qn_c, kn_c, vnt_c)


def _diff_body(q_ref, key_blocks, val_blocks, lam_ref, g_ref, o_ref, s1_ref, s2_ref, lam_init):
    q = q_ref[0, 0]
    lane = lax.broadcasted_iota(jnp.int32, q.shape, 1)
    zero = jnp.zeros_like(q)
    q1 = jnp.where(lane < DIFF_QK_DIM, q, zero)
    q2 = jnp.where((lane >= DIFF_QK_DIM) & (lane < 2 * DIFF_QK_DIM), q, zero)
    n_blk = len(key_blocks)

    m1 = m2 = None
    for c, kb in enumerate(key_blocks):
        k = kb()
        rows = slice(c * KEY_BLK, (c + 1) * KEY_BLK)
        sa = _dot_nt(k, q1)
        sb = _dot_nt(k, q2)
        s1_ref[rows, :] = sa
        s2_ref[rows, :] = sb
        ma = jnp.max(sa, axis=0, keepdims=True)
        mb = jnp.max(sb, axis=0, keepdims=True)
        m1 = ma if m1 is None else jnp.maximum(m1, ma)
        m2 = mb if m2 is None else jnp.maximum(m2, mb)

    l1 = l2 = None
    for c in range(n_blk):
        rows = slice(c * KEY_BLK, (c + 1) * KEY_BLK)
        pa = jnp.exp(s1_ref[rows, :] - m1)
        pb = jnp.exp(s2_ref[rows, :] - m2)
        s1_ref[rows, :] = pa
        s2_ref[rows, :] = pb
        sa = jnp.sum(pa, axis=0, keepdims=True)
        sb = jnp.sum(pb, axis=0, keepdims=True)
        l1 = sa if l1 is None else l1 + sa
        l2 = sb if l2 is None else l2 + sb

    lp = lam_ref[...]
    lam = (jnp.exp(jnp.sum(lp[0:1] * lp[1:2], axis=1, keepdims=True))
           - jnp.exp(jnp.sum(lp[2:3] * lp[3:4], axis=1, keepdims=True)) + lam_init)
    a1 = 1.0 / l1
    a2 = lam / l2

    acc = None
    for c, vb in enumerate(val_blocks):
        rows = slice(c * KEY_BLK, (c + 1) * KEY_BLK)
        w = (s1_ref[rows, :] * a1 - s2_ref[rows, :] * a2).astype(BF16)
        part = _dot(vb(), w)
        acc = part if acc is None else acc + part

    ms = jnp.mean(acc * acc, axis=0, keepdims=True)
    o_ref[0, 0] = acc * lax.rsqrt(ms + EPS) * (g_ref[...] * (1.0 - lam_init))


def _diff_lat_kernel(q_ref, kc_ref, k_ref, vc_ref, v_ref, lam_ref, g_ref, o_ref, s1_ref, s2_ref, *, lam_init):
    n_lat = k_ref.shape[2] // KEY_BLK
    keys = [lambda: kc_ref[0, 0]] + [
        functools.partial(lambda c: k_ref[0, 0, c * KEY_BLK:(c + 1) * KEY_BLK, :], c) for c in range(n_lat)]
    vals = [lambda: vc_ref[0, 0]] + [
        functools.partial(lambda c: v_ref[0, 0, :, c * KEY_BLK:(c + 1) * KEY_BLK], c) for c in range(n_lat)]
    _diff_body(q_ref, keys, vals, lam_ref, g_ref, o_ref, s1_ref, s2_ref, lam_init)


def _diff_ctx_kernel(q_ref, kc_ref, vc_ref, lam_ref, g_ref, o_ref, s1_ref, s2_ref, *, lam_init):
    _diff_body(q_ref, [lambda: kc_ref[0, 0]], [lambda: vc_ref[0, 0]], lam_ref, g_ref, o_ref,
               s1_ref, s2_ref, lam_init)


def _diff_lat(qd, kd_c, kd, vdt_c, vdt, lam_p, g_col, lam_init, qb):
    b, _, t, _ = qd.shape
    c = kd_c.shape[2]
    one_head = lambda shape, idx: pl.BlockSpec((1, 1) + shape, idx)
    return pl.pallas_call(
        functools.partial(_diff_lat_kernel, lam_init=lam_init),
        grid=(b, DIFF_HEADS, t // qb),
        in_specs=[one_head((qb, LANES), lambda bi, h, j: (bi, h, j, 0)),
                  one_head((c, LANES), lambda bi, h, j: (bi, h, 0, 0)),
                  one_head((t, LANES), lambda bi, h, j: (bi, h, 0, 0)),
                  one_head((DIFF_V_DIM, c), lambda bi, h, j: (bi, h, 0, 0)),
                  one_head((DIFF_V_DIM, t), lambda bi, h, j: (bi, h, 0, 0)),
                  pl.BlockSpec(lam_p.shape, lambda bi, h, j: (0, 0)),
                  pl.BlockSpec(g_col.shape, lambda bi, h, j: (0, 0))],
        out_specs=one_head((DIFF_V_DIM, qb), lambda bi, h, j: (bi, h, 0, j)),
        out_shape=jax.ShapeDtypeStruct((b, DIFF_HEADS, DIFF_V_DIM, t), F32),
        scratch_shapes=[pltpu.VMEM((c + t, qb), F32), pltpu.VMEM((c + t, qb), F32)],
        compiler_params=_cparams(3),
        name="diff_lat",
    )(qd, kd_c, kd, vdt_c, vdt, lam_p, g_col)


def _diff_ctx(qd_c, kd_c, vdt_c, lam_p, g_col, lam_init):
    b, _, c, _ = qd_c.shape
    one_head = lambda shape: pl.BlockSpec((1, 1) + shape, lambda bi, h: (bi, h, 0, 0))
    return pl.pallas_call(
        functools.partial(_diff_ctx_kernel, lam_init=lam_init),
        grid=(b, DIFF_HEADS),
        in_specs=[one_head((c, LANES)), one_head((c, LANES)), one_head((DIFF_V_DIM, c)),
                  pl.BlockSpec(lam_p.shape, lambda bi, h: (0, 0)),
                  pl.BlockSpec(g_col.shape, lambda bi, h: (0, 0))],
        out_specs=one_head((DIFF_V_DIM, c)),
        out_shape=jax.ShapeDtypeStruct((b, DIFF_HEADS, DIFF_V_DIM, c), F32),
        scratch_shapes=[pltpu.VMEM((c, c), F32), pltpu.VMEM((c, c), F32)],
        compiler_params=_cparams(2),
        name="diff_ctx",
    )(qd_c, kd_c, vdt_c, lam_p, g_col)


CONV_PAD = 16
CONV_TILE = 256


def _conv_kernel(u_ref, w_ref, b_ref, g_ref, beta_ref, o_ref, pad_ref):
    s = u_ref.shape[1]
    zeros = jnp.zeros((CONV_PAD, CONV_WIDTH), F32)
    pad_ref[0:CONV_PAD, :] = zeros
    pad_ref[CONV_PAD + s:CONV_PAD + s + CONV_PAD, :] = zeros
    pad_ref[CONV_PAD:CONV_PAD + s, :] = u_ref[0]
    shift = CONV_PAD - CONV_TAPS // 2
    for t0 in range(0, s, CONV_TILE):
        acc = jnp.broadcast_to(b_ref[...], (CONV_TILE, CONV_WIDTH))
        for j in range(CONV_TAPS):
            acc = acc + pad_ref[t0 + j + shift:t0 + j + shift + CONV_TILE, :] * w_ref[j:j + 1, :]
        mu = jnp.mean(acc, axis=-1, keepdims=True)
        d = acc - mu
        var = jnp.mean(d * d, axis=-1, keepdims=True)
        y = d * lax.rsqrt(var + EPS) * g_ref[...] + beta_ref[...]
        o_ref[0, t0:t0 + CONV_TILE, :] = _silu(y)


def _conv(u, conv_w, conv_b, ln_g, ln_b):
    b, s, _ = u.shape
    row = lambda a: a.reshape(1, CONV_WIDTH)
    vec_spec = pl.BlockSpec((1, CONV_WIDTH), lambda bi: (0, 0))
    return pl.pallas_call(
        _conv_kernel,
        grid=(b,),
        in_specs=[pl.BlockSpec((1, s, CONV_WIDTH), lambda bi: (bi, 0, 0)),
                  pl.BlockSpec((CONV_TAPS, CONV_WIDTH), lambda bi: (0, 0)),
                  vec_spec, vec_spec, vec_spec],
        out_specs=pl.BlockSpec((1, s, CONV_WIDTH), lambda bi: (bi, 0, 0)),
        out_shape=jax.ShapeDtypeStruct((b, s, CONV_WIDTH), F32),
        scratch_shapes=[pltpu.VMEM((s + 2 * CONV_PAD, CONV_WIDTH), F32)],
        compiler_params=_cparams(1),
        name="conformer_conv",
    )(u, conv_w, row(conv_b), row(ln_g), row(ln_b))


def _outproj_kernel(ont_ref, odt_ref, cv_ref, gate_ref, x_ref, w_ref, gpost_ref, gmod_ref, o_ref):
    tm = x_ref.shape[1]
    a = ont_ref[0].reshape(NA_WIDTH, tm).T
    bq = odt_ref[0].reshape(DIFF_WIDTH, tm).T
    mixed = (jnp.concatenate([a, bq, cv_ref[0]], axis=1) * gate_ref[0]).astype(BF16)
    y = _dot(mixed, w_ref[...])
    ms = jnp.mean(y * y, axis=-1, keepdims=True)
    o_ref[0] = x_ref[0] + gmod_ref[0] * (y * lax.rsqrt(ms + EPS) * gpost_ref[...])


def _outproj(ont, odt, cv, gate, x, w, g_post, gmod, tm):
    b, s, _ = x.shape
    gm_idx = (lambda bi, i: (bi, 0, 0)) if gmod.shape[0] == b else (lambda bi, i: (0, 0, 0))
    return pl.pallas_call(
        _outproj_kernel,
        grid=(b, s // tm),
        in_specs=[pl.BlockSpec((1, NA_PAIRS, LANES, tm), lambda bi, i: (bi, 0, 0, i)),
                  pl.BlockSpec((1, DIFF_HEADS, DIFF_V_DIM, tm), lambda bi, i: (bi, 0, 0, i)),
                  pl.BlockSpec((1, tm, CONV_WIDTH), lambda bi, i: (bi, i, 0)),
                  pl.BlockSpec((1, tm, D_MODEL), lambda bi, i: (bi, i, 0)),
                  pl.BlockSpec((1, tm, D_MODEL), lambda bi, i: (bi, i, 0)),
                  pl.BlockSpec((D_MODEL, D_MODEL), lambda bi, i: (0, 0)),
                  pl.BlockSpec((1, D_MODEL), lambda bi, i: (0, 0)),
                  pl.BlockSpec((1, 1, D_MODEL), gm_idx)],
        out_specs=pl.BlockSpec((1, tm, D_MODEL), lambda bi, i: (bi, i, 0)),
        out_shape=jax.ShapeDtypeStruct((b, s, D_MODEL), F32),
        compiler_params=_cparams(2),
        name="outproj",
    )(ont, odt, cv, gate, x, w, g_post.reshape(1, D_MODEL), gmod)


def _relayout_w_in(w_in):
    edges = np.cumsum((0, NA_WIDTH, NA_WIDTH, DIFF_WIDTH, DIFF_WIDTH, NA_WIDTH, NA_WIDTH, DIFF_WIDTH,
                       DIFF_WIDTH, CONV_WIDTH, CONV_WIDTH, CONV_WIDTH))
    na_k, na_v, df_k, df_v, na_q, na_g, df_q, df_g, cv_a, cv_b, cv_g = [
        w_in[..., a:b] for a, b in zip(edges[:-1], edges[1:])]

    def pad_heads(w):
        w = w.reshape(w.shape[:-1] + (DIFF_HEADS, DIFF_V_DIM))
        w = jnp.pad(w, ((0, 0),) * (w.ndim - 1) + ((0, LANES - DIFF_V_DIM),))
        return w.reshape(w.shape[:-2] + (DIFF_HEADS * LANES,))

    return jnp.concatenate([na_k, pad_heads(df_k), na_q, pad_heads(df_q), na_v, df_v,
                            na_g, df_g, cv_g, cv_a, cv_b], axis=-1).astype(BF16)


IN_TM = 256
OUT_TM = 256
DIFF_QB = 256


def kernel(x, c, ctx, c_ctx, w_mod, b_mod, g_pre, g_post, w_in, w_out, na_rpb, diff_lq1, diff_lk1,
           diff_lq2, diff_lk2, diff_subln_g, conv_w, conv_b, conv_ln_g, conv_ln_b):
    batch, seq, _ = x.shape
    pad_rows = (-(batch + 1)) % 8
    cc = jnp.concatenate([c, c_ctx[None, :], jnp.zeros((pad_rows, D_MODEL), F32)], axis=0)
    mod = _modulation(cc, w_mod, b_mod)
    bias = _na_bias_tables(na_rpb.reshape(-1))
    w_in_b = _relayout_w_in(w_in)
    w_out_b = w_out.astype(BF16)
    tables = _rope_tables(seq)
    lam_p = jnp.stack([diff_lq1, diff_lk1, diff_lq2, diff_lk2], axis=1)

    xc = ctx
    for l in range(DEPTH):
        update_ctx = l < DEPTH - 1
        lam_init = 0.8 - 0.6 * math.exp(-0.3 * l)
        ss_lat = mod[l, :batch, :2 * D_MODEL].reshape(batch, 1, 1, 2 * D_MODEL)
        ss_ctx = mod[l, batch, :2 * D_MODEL].reshape(1, 1, 1, 2 * D_MODEL)
        g_col = diff_subln_g[l].reshape(DIFF_V_DIM, 1)

        kn, kd, qn, qd, vnt, vdt, gate, u = _inproj(x, ss_lat, g_pre[l], w_in_b[l], tables, IN_TM)
        kn_c, kd_c, qn_c, qd_c, vnt_c, vdt_c, gate_c, u_c = _inproj(
            xc, ss_ctx, g_pre[l], w_in_b[l], None, xc.shape[1])

        ont = _na_lat(qn, kn_c, kn, vnt_c, vnt, bias[l])
        odt = _diff_lat(qd, kd_c, kd, vdt_c, vdt, lam_p[l], g_col, lam_init, DIFF_QB)
        cv = _conv(u, conv_w[l], conv_b[l], conv_ln_g[l], conv_ln_b[l])
        gmod = mod[l, :batch, 2 * D_MODEL:].reshape(batch, 1, D_MODEL)
        x_new = _outproj(ont, odt, cv, gate, x, w_out_b[l], g_post[l], gmod, OUT_TM)

        if update_ctx:
            ont_c = _na_ctx(qn_c, kn_c, vnt_c)
            odt_c = _diff_ctx(qd_c, kd_c, vdt_c, lam_p[l], g_col, lam_init)
            cv_c = _conv(u_c, conv_w[l], conv_b[l], conv_ln_g[l], conv_ln_b[l])
            gmod_c = mod[l, batch, 2 * D_MODEL:].reshape(1, 1, D_MODEL)
            xc = _outproj(ont_c, odt_c, cv_c, gate_c, xc, w_out_b[l], g_post[l], gmod_c, xc.shape[1])
        x = x_new
    return x
```

```python
import functools
import math

import numpy as np
import jax
import jax.numpy as jnp
from jax import lax
from jax.experimental import pallas as pl
from jax.experimental.pallas import tpu as pltpu

F32 = jnp.float32
BF16 = jnp.bfloat16

D_MODEL = 1024
DEPTH = 4
GRID_W = 64
NA_HEADS = 6
NA_HEAD_DIM = 64
NA_WIDTH = NA_HEADS * NA_HEAD_DIM
NA_WIN_H = 8
NA_WIN_W = 16
DIFF_HEADS = 4
DIFF_QK_DIM = 48
DIFF_V_DIM = 2 * DIFF_QK_DIM
DIFF_WIDTH = DIFF_HEADS * DIFF_V_DIM
CONV_WIDTH = D_MODEL - NA_WIDTH - DIFF_WIDTH
CONV_TAPS = 31
ROPE_BASE = 10000.0
EPS = 1e-6
NEG_INF = -1e30
LOG2_E = math.log2(math.e)
DIFF_Q_SCALE = DIFF_QK_DIM ** -0.5 * LOG2_E
NA_Q_SCALE = NA_HEAD_DIM ** -0.5 * LOG2_E

LANES = 128
NA_PAIRS = NA_HEADS // 2
KEY_BLK = 256
NA_QBLK = 256
NA_ROWS_PER_QBLK = NA_QBLK // GRID_W
NA_LOC_BLKS = 3
NA_PATTERNS = 3
VMEM_LIMIT = 56 * 1024 * 1024

C_KN = 0
C_KD = C_KN + NA_WIDTH
C_QN = C_KD + DIFF_HEADS * LANES
C_QD = C_QN + NA_WIDTH
C_VN = C_QD + DIFF_HEADS * LANES
C_VD = C_VN + NA_WIDTH
C_GATE = C_VD + DIFF_WIDTH
C_CA = C_GATE + D_MODEL
C_CB = C_CA + CONV_WIDTH
IN_COLS_PADDED = C_CB + CONV_WIDTH


def _cparams(n_axes, vmem=VMEM_LIMIT):
    return pltpu.CompilerParams(dimension_semantics=("arbitrary",) * n_axes, vmem_limit_bytes=vmem)


def _silu(z):
    return z * jax.nn.sigmoid(z)


def _dot(a, b):
    return jnp.dot(a, b, preferred_element_type=F32)


def _dot_nt(a, b):
    return lax.dot_general(a, b, (((1,), (1,)), ((), ())), preferred_element_type=F32)


MXU_SKEW = 1


def _pipelined(tasks, skew):
    pending = []
    for produce, consume in tasks:
        pending.append((consume, produce()))
        if len(pending) > skew:
            done, value = pending.pop(0)
            done(value)
    for done, value in pending:
        done(value)


def _mod_kernel(cc_ref, w_ref, b_ref, o_ref):
    s = _silu(cc_ref[...])
    o_ref[0] = _dot(s, w_ref[0]) + b_ref[0]


def _modulation(cc, w_mod, b_mod):
    n_rows = cc.shape[0]
    return pl.pallas_call(
        _mod_kernel,
        grid=(DEPTH, 3),
        in_specs=[
            pl.BlockSpec((n_rows, D_MODEL), lambda l, j: (0, 0)),
            pl.BlockSpec((1, D_MODEL, D_MODEL), lambda l, j: (l, 0, j)),
            pl.BlockSpec((1, 1, D_MODEL), lambda l, j: (l, 0, j)),
        ],
        out_specs=pl.BlockSpec((1, n_rows, D_MODEL), lambda l, j: (l, 0, j)),
        out_shape=jax.ShapeDtypeStruct((DEPTH, n_rows, 3 * D_MODEL), F32),
        compiler_params=_cparams(2),
        name="adaln_mod",
    )(cc, w_mod, b_mod.reshape(DEPTH, 1, 3 * D_MODEL))


def _na_bias_kernel(rpb_ref, o_ref):
    l = pl.program_id(0)
    h = pl.program_id(1)
    n_dr = 2 * NA_WIN_H - 1
    n_dc = 2 * NA_WIN_W - 1
    base = (l * NA_HEADS + h) * (n_dr * n_dc)
    kc = lax.broadcasted_iota(jnp.int32, (GRID_W, LANES), 0)
    lane = lax.broadcasted_iota(jnp.int32, (GRID_W, LANES), 1)
    qc = lane % GRID_W
    cstart = jnp.clip(qc - NA_WIN_W // 2, 0, GRID_W - NA_WIN_W)
    col_ok = (kc >= cstart) & (kc < cstart + NA_WIN_W)
    dc = kc - qc + (NA_WIN_W - 1)
    neg = jnp.full((GRID_W, LANES), NEG_INF, F32)
    dc_masks = [col_ok & (dc == i) for i in range(n_dc)]
    tiles = []
    for dr in range(n_dr):
        t = neg
        for i in range(n_dc):
            t = jnp.where(dc_masks[i], rpb_ref[base + dr * n_dc + i], t)
        tiles.append(t * LOG2_E)
    left = lane < GRID_W

    def tile(dr, ok):
        return tiles[dr] if ok else neg

    patterns = (
        (lambda i: (0, NA_WIN_H), NA_WIN_H - 1),
        (lambda i: (i, i + NA_WIN_H), NA_WIN_H - 1 - NA_ROWS_PER_QBLK),
        (lambda i: (4, 4 + NA_WIN_H), -1),
    )
    for p, (rng, off) in enumerate(patterns):
        for j in range(NA_LOC_BLKS * KEY_BLK // GRID_W):
            halves = []
            for i0 in (0, 2):
                pair = []
                for i in (i0, i0 + 1):
                    lo, hi = rng(i)
                    pair.append(tile(j - i + off, lo <= j < hi) if 0 <= j - i + off < n_dr else neg)
                halves.append(jnp.where(left, pair[0], pair[1]))
            o_ref[0, 0, p, j * GRID_W:(j + 1) * GRID_W, :] = jnp.concatenate(halves, axis=1)


def _na_bias_tables(rpb_flat):
    n_keys = NA_LOC_BLKS * KEY_BLK
    return pl.pallas_call(
        _na_bias_kernel,
        grid=(DEPTH, NA_HEADS),
        in_specs=[pl.BlockSpec(memory_space=pltpu.SMEM)],
        out_specs=pl.BlockSpec((1, 1, NA_PATTERNS, n_keys, NA_QBLK), lambda l, h: (l, h, 0, 0, 0)),
        out_shape=jax.ShapeDtypeStruct((DEPTH, NA_HEADS, NA_PATTERNS, n_keys, NA_QBLK), F32),
        compiler_params=_cparams(2),
        name="na_bias_tables",
    )(rpb_flat)


def _rope_tables(seq):
    half = DIFF_QK_DIM // 2
    t = np.arange(seq)
    pos = np.stack([t // GRID_W, t % GRID_W], axis=1).astype(np.float64)
    lane = np.arange(LANES)
    live = lane < DIFF_V_DIM
    seg = np.where(live, lane // half, 0)
    i = lane % half
    inv = ROPE_BASE ** (-(2.0 * (i % (half // 2))) / half)
    ang = pos[:, seg % 2] * inv[None, :]
    cos = np.where(live[None, :], np.cos(ang), 1.0)
    sgn = np.where(i < half // 2, -1.0, 1.0)
    sin = np.where(live[None, :], np.sin(ang) * sgn[None, :], 0.0)
    return jnp.asarray(cos, F32), jnp.asarray(sin, F32)


def _inproj_kernel(*refs, rope):
    if rope:
        x_ref, ss_ref, gpre_ref, w_ref, cos_ref, sin_ref = refs[:6]
        outs = refs[6:]
    else:
        x_ref, ss_ref, gpre_ref, w_ref = refs[:4]
        outs = refs[4:]
    kn_ref, kd_ref, qn_ref, qd_ref, vnt_ref, vdt_ref, gate_ref, u_ref = outs

    x = x_ref[0]
    ms = jnp.mean(x * x, axis=-1, keepdims=True)
    hn = x * lax.rsqrt(ms + EPS) * gpre_ref[...]
    ss = ss_ref[0, 0]
    hb = (hn * (1.0 + ss[:, D_MODEL:]) + ss[:, :D_MODEL]).astype(BF16)

    def proj(a, b):
        return _dot(hb, w_ref[:, a:b])

    if rope:
        cos = cos_ref[...]
        sin = sin_ref[...]
        lane = lax.broadcasted_iota(jnp.int32, cos.shape, 1)
        quarter = DIFF_QK_DIM // 4
        first = (lane % (2 * quarter)) < quarter

        def rot(zb):
            up = pltpu.roll(zb, LANES - quarter, axis=1)
            dn = pltpu.roll(zb, quarter, axis=1)
            return zb * cos + jnp.where(first, up, dn) * sin
    else:
        def rot(zb):
            return zb

    def store_kn(z):
        for p in range(NA_PAIRS):
            kn_ref[0, p] = z[:, p * LANES:(p + 1) * LANES].astype(BF16)

    def store_kd(z):
        for h in range(DIFF_HEADS):
            kd_ref[0, h] = rot(z[:, h * LANES:(h + 1) * LANES]).astype(BF16)

    def store_qn(z):
        for p in range(NA_PAIRS):
            qn_ref[0, p] = (z[:, p * LANES:(p + 1) * LANES] * NA_Q_SCALE).astype(BF16)

    def store_qd(z):
        for h in range(DIFF_HEADS):
            qd_ref[0, h] = (rot(z[:, h * LANES:(h + 1) * LANES]) * DIFF_Q_SCALE).astype(BF16)

    def store_vn(z):
        zt = z.T
        for p in range(NA_PAIRS):
            vnt_ref[0, p] = zt[p * LANES:(p + 1) * LANES].astype(BF16)

    def store_vd(z):
        zt = z.T
        for h in range(DIFF_HEADS):
            vdt_ref[0, h] = zt[h * DIFF_V_DIM:(h + 1) * DIFF_V_DIM].astype(BF16)

    def store_gate(z):
        gate_ref[0] = _silu(z)

    def store_u(ab):
        u_ref[0] = ab[0] * jax.nn.sigmoid(ab[1])

    _pipelined([
        (lambda: proj(C_KN, C_KD), store_kn),
        (lambda: proj(C_KD, C_QN), store_kd),
        (lambda: proj(C_QN, C_QD), store_qn),
        (lambda: proj(C_QD, C_VN), store_qd),
        (lambda: proj(C_VN, C_VD), store_vn),
        (lambda: proj(C_VD, C_GATE), store_vd),
        (lambda: proj(C_GATE, C_CA), store_gate),
        (lambda: (proj(C_CA, C_CB), proj(C_CB, IN_COLS_PADDED)), store_u),
    ], MXU_SKEW)


def _inproj(x, ss, g_pre, w, tables, tm):
    b, s, _ = x.shape
    rope = tables is not None
    ss_idx = (lambda bi, i: (bi, 0, 0, 0)) if ss.shape[0] == b else (lambda bi, i: (0, 0, 0, 0))
    in_specs = [
        pl.BlockSpec((1, tm, D_MODEL), lambda bi, i: (bi, i, 0)),
        pl.BlockSpec((1, 1, 1, 2 * D_MODEL), ss_idx),
        pl.BlockSpec((1, D_MODEL), lambda bi, i: (0, 0)),
        pl.BlockSpec((D_MODEL, IN_COLS_PADDED), lambda bi, i: (0, 0)),
    ]
    args = [x, ss, g_pre.reshape(1, D_MODEL), w]
    if rope:
        in_specs += [pl.BlockSpec((tm, LANES), lambda bi, i: (i, 0))] * 2
        args += list(tables)
    row_major = lambda n: pl.BlockSpec((1, n, tm, LANES), lambda bi, i: (bi, 0, i, 0))
    out_specs = [
        row_major(NA_PAIRS), row_major(DIFF_HEADS), row_major(NA_PAIRS), row_major(DIFF_HEADS),
        pl.BlockSpec((1, NA_PAIRS, LANES, tm), lambda bi, i: (bi, 0, 0, i)),
        pl.BlockSpec((1, DIFF_HEADS, DIFF_V_DIM, tm), lambda bi, i: (bi, 0, 0, i)),
        pl.BlockSpec((1, tm, D_MODEL), lambda bi, i: (bi, i, 0)),
        pl.BlockSpec((1, tm, CONV_WIDTH), lambda bi, i: (bi, i, 0)),
    ]
    out_shape = [
        jax.ShapeDtypeStruct((b, NA_PAIRS, s, LANES), BF16),
        jax.ShapeDtypeStruct((b, DIFF_HEADS, s, LANES), BF16),
        jax.ShapeDtypeStruct((b, NA_PAIRS, s, LANES), BF16),
        jax.ShapeDtypeStruct((b, DIFF_HEADS, s, LANES), BF16),
        jax.ShapeDtypeStruct((b, NA_PAIRS, LANES, s), BF16),
        jax.ShapeDtypeStruct((b, DIFF_HEADS, DIFF_V_DIM, s), BF16),
        jax.ShapeDtypeStruct((b, s, D_MODEL), F32),
        jax.ShapeDtypeStruct((b, s, CONV_WIDTH), F32),
    ]
    return pl.pallas_call(
        functools.partial(_inproj_kernel, rope=rope),
        grid=(b, s // tm),
        in_specs=in_specs,
        out_specs=out_specs,
        out_shape=out_shape,
        compiler_params=_cparams(2),
        name="inproj_lat" if rope else "inproj_ctx",
    )(*args)


def _na_heads(q_ref, k_refs, vt_refs, bias_ref, o_ref, s_ref):
    lane = lax.broadcasted_iota(jnp.int32, (NA_QBLK, LANES), 1)

    def scores(idx, p, hh):
        q = q_ref[0, p]
        in_head = (lane >= hh * NA_HEAD_DIM) & (lane < (hh + 1) * NA_HEAD_DIM)
        qm = jnp.where(in_head, q, jnp.zeros_like(q))
        keys = [k_ref[0, p] for k_ref in k_refs]
        s = _dot_nt(keys[0] if len(keys) == 1 else jnp.concatenate(keys, axis=0), qm)
        s_ctx = s[:KEY_BLK]
        s_ref[idx % 2, 0:KEY_BLK, :] = s_ctx
        m = jnp.max(s_ctx, axis=0, keepdims=True)
        if len(keys) > 1:
            s_loc = s[KEY_BLK:] + bias_ref[2 * p + hh, 0]
            s_ref[idx % 2, KEY_BLK:, :] = s_loc
            m = jnp.maximum(m, jnp.max(s_loc, axis=0, keepdims=True))
        return m

    def finish(idx, p, hh, m):
        rows = slice(hh * NA_HEAD_DIM, (hh + 1) * NA_HEAD_DIM)
        denom = o = None
        for c, vt_ref in enumerate(vt_refs):
            e = jnp.exp2(s_ref[idx % 2, c * KEY_BLK:(c + 1) * KEY_BLK, :] - m)
            dc = jnp.sum(e, axis=0, keepdims=True)
            oc = _dot(vt_ref[0, p, rows, :], e.astype(BF16))
            denom = dc if denom is None else denom + dc
            o = oc if o is None else o + oc
        o_ref[0, p, rows, :] = o * (1.0 / denom)

    _pipelined([(functools.partial(scores, 2 * p + hh, p, hh), functools.partial(finish, 2 * p + hh, p, hh))
                for p in range(NA_PAIRS) for hh in range(2)], MXU_SKEW)


def _na_lat_kernel(q_ref, kc_ref, k0_ref, k1_ref, k2_ref, vc_ref, v0_ref, v1_ref, v2_ref, bias_ref, o_ref, s_ref):
    _na_heads(q_ref, (kc_ref, k0_ref, k1_ref, k2_ref), (vc_ref, v0_ref, v1_ref, v2_ref), bias_ref, o_ref, s_ref)


def _na_ctx_kernel(q_ref, kc_ref, vc_ref, o_ref, s_ref):
    _na_heads(q_ref, (kc_ref,), (vc_ref,), None, o_ref, s_ref)


def _na_lat(qn, kn_c, kn, vnt_c, vnt, bias):
    b, _, t, _ = qn.shape
    n_qblk = t // NA_QBLK
    last_start = t // KEY_BLK - NA_LOC_BLKS

    def start(j):
        return jnp.clip(j - 1, 0, last_start)

    def pattern(j):
        return jnp.where(j == 0, 0, jnp.where(j == n_qblk - 1, 2, 1))

    row_blk = (1, NA_PAIRS, KEY_BLK, LANES)
    col_blk = (1, NA_PAIRS, LANES, KEY_BLK)
    k_specs = [pl.BlockSpec(row_blk, functools.partial(lambda bi, j, i: (bi, 0, start(j) + i, 0), i=i))
               for i in range(NA_LOC_BLKS)]
    v_specs = [pl.BlockSpec(col_blk, functools.partial(lambda bi, j, i: (bi, 0, 0, start(j) + i), i=i))
               for i in range(NA_LOC_BLKS)]
    return pl.pallas_call(
        _na_lat_kernel,
        grid=(b, n_qblk),
        in_specs=[pl.BlockSpec(row_blk, lambda bi, j: (bi, 0, j, 0)),
                  pl.BlockSpec(row_blk, lambda bi, j: (bi, 0, 0, 0))] + k_specs
                 + [pl.BlockSpec(col_blk, lambda bi, j: (bi, 0, 0, 0))] + v_specs
                 + [pl.BlockSpec((NA_HEADS, 1, NA_LOC_BLKS * KEY_BLK, NA_QBLK),
                                 lambda bi, j: (0, pattern(j), 0, 0))],
        out_specs=pl.BlockSpec(col_blk, lambda bi, j: (bi, 0, 0, j)),
        out_shape=jax.ShapeDtypeStruct((b, NA_PAIRS, LANES, t), F32),
        scratch_shapes=[pltpu.VMEM((2, (1 + NA_LOC_BLKS) * KEY_BLK, NA_QBLK), F32)],
        compiler_params=_cparams(2),
        name="na_lat",
    )(qn, kn_c, kn, kn, kn, vnt_c, vnt, vnt, vnt, bias)


def _na_ctx(qn_c, kn_c, vnt_c):
    b, _, c, _ = qn_c.shape
    row_blk = (1, NA_PAIRS, c, LANES)
    col_blk = (1, NA_PAIRS, LANES, c)
    return pl.pallas_call(
        _na_ctx_kernel,
        grid=(b,),
        in_specs=[pl.BlockSpec(row_blk, lambda bi: (bi, 0, 0, 0)),
                  pl.BlockSpec(row_blk, lambda bi: (bi, 0, 0, 0)),
                  pl.BlockSpec(col_blk, lambda bi: (bi, 0, 0, 0))],
        out_specs=pl.BlockSpec(col_blk, lambda bi: (bi, 0, 0, 0)),
        out_shape=jax.ShapeDtypeStruct((b, NA_PAIRS, LANES, c), F32),
        scratch_shapes=[pltpu.VMEM((2, c, c), F32)],
        compiler_params=_cparams(1),
        name="na_ctx",
    )(qn_c, kn_c, vnt_c)


DIFF_SUB = 256
PV_LAG = 4


def _diff_body(q_ref, key_blocks, val_blocks, lam_ref, g_ref, o_ref, s_ref, lam_init):
    n_sub = q_ref.shape[2] // DIFF_SUB
    lp = lam_ref[...]
    lam = (jnp.exp(jnp.sum(lp[0:1] * lp[1:2], axis=1, keepdims=True))
           - jnp.exp(jnp.sum(lp[2:3] * lp[3:4], axis=1, keepdims=True)) + lam_init)
    gain = g_ref[...] * (1.0 - lam_init)

    def masked_queries(i):
        q = q_ref[0, 0, i * DIFF_SUB:(i + 1) * DIFF_SUB, :]
        lane = lax.broadcasted_iota(jnp.int32, q.shape, 1)
        zero = jnp.zeros_like(q)
        return (jnp.where(lane < DIFF_QK_DIM, q, zero),
                jnp.where((lane >= DIFF_QK_DIM) & (lane < 2 * DIFF_QK_DIM), q, zero))

    def score_block(i, c, qm, m):
        k = key_blocks[c]()
        for mp in range(2):
            s = _dot_nt(k, qm[mp])
            s_ref[i % 2, mp, c * KEY_BLK:(c + 1) * KEY_BLK, :] = s
            mc = jnp.max(s, axis=0, keepdims=True)
            m[mp] = mc if m[mp] is None else jnp.maximum(m[mp], mc)

    def exp_block(i, c, m, l):
        ps = []
        for mp in range(2):
            p = jnp.exp2(s_ref[i % 2, mp, c * KEY_BLK:(c + 1) * KEY_BLK, :] - m[mp])
            lc = jnp.sum(p, axis=0, keepdims=True)
            l[mp] = lc if l[mp] is None else l[mp] + lc
            ps.append(p.astype(BF16))
        return ps

    def pv_block(c, ps, acc):
        vt = val_blocks[c]()
        for mp in range(2):
            pv = _dot(vt, ps[mp])
            acc[mp] = pv if acc[mp] is None else acc[mp] + pv

    n_blk = len(key_blocks)
    m_next = [None, None]
    qm = masked_queries(0)
    for c in range(n_blk):
        score_block(0, c, qm, m_next)
    for i in range(n_sub):
        m, m_next = m_next, [None, None]
        l, acc = [None, None], [None, None]
        if i + 1 < n_sub:
            qm = masked_queries(i + 1)
        waiting = []
        for c in range(n_blk):
            if i + 1 < n_sub:
                score_block(i + 1, c, qm, m_next)
            waiting.append((c, exp_block(i, c, m, l)))
            if len(waiting) > PV_LAG:
                pv_block(*waiting.pop(0), acc)
        for item in waiting:
            pv_block(*item, acc)
        out = acc[0] * (1.0 / l[0]) - acc[1] * (lam / l[1])
        ms = jnp.mean(out * out, axis=0, keepdims=True)
        o_ref[0, 0, :, i * DIFF_SUB:(i + 1) * DIFF_SUB] = out * lax.rsqrt(ms + EPS) * gain


def _diff_lat_kernel(q_ref, kc_ref, k_ref, vc_ref, v_ref, lam_ref, g_ref, o_ref, s_ref, *, lam_init):
    n_lat = k_ref.shape[2] // KEY_BLK
    keys = [lambda: kc_ref[0, 0]] + [
        functools.partial(lambda c: k_ref[0, 0, c * KEY_BLK:(c + 1) * KEY_BLK, :], c) for c in range(n_lat)]
    vals = [lambda: vc_ref[0, 0]] + [
        functools.partial(lambda c: v_ref[0, 0, :, c * KEY_BLK:(c + 1) * KEY_BLK], c) for c in range(n_lat)]
    _diff_body(q_ref, keys, vals, lam_ref, g_ref, o_ref, s_ref, lam_init)


def _diff_ctx_kernel(q_ref, kc_ref, vc_ref, lam_ref, g_ref, o_ref, s_ref, *, lam_init):
    _diff_body(q_ref, [lambda: kc_ref[0, 0]], [lambda: vc_ref[0, 0]], lam_ref, g_ref, o_ref, s_ref, lam_init)


def _diff_scratch(n_keys):
    return [pltpu.VMEM((2, 2, n_keys, DIFF_SUB), F32)]


def _diff_lat(qd, kd_c, kd, vdt_c, vdt, lam_p, g_col, lam_init, qb):
    b, _, t, _ = qd.shape
    c = kd_c.shape[2]
    one_head = lambda shape, idx: pl.BlockSpec((1, 1) + shape, idx)
    return pl.pallas_call(
        functools.partial(_diff_lat_kernel, lam_init=lam_init),
        grid=(b, DIFF_HEADS, t // qb),
        in_specs=[one_head((qb, LANES), lambda bi, h, j: (bi, h, j, 0)),
                  one_head((c, LANES), lambda bi, h, j: (bi, h, 0, 0)),
                  one_head((t, LANES), lambda bi, h, j: (bi, h, 0, 0)),
                  one_head((DIFF_V_DIM, c), lambda bi, h, j: (bi, h, 0, 0)),
                  one_head((DIFF_V_DIM, t), lambda bi, h, j: (bi, h, 0, 0)),
                  pl.BlockSpec(lam_p.shape, lambda bi, h, j: (0, 0)),
                  pl.BlockSpec(g_col.shape, lambda bi, h, j: (0, 0))],
        out_specs=one_head((DIFF_V_DIM, qb), lambda bi, h, j: (bi, h, 0, j)),
        out_shape=jax.ShapeDtypeStruct((b, DIFF_HEADS, DIFF_V_DIM, t), F32),
        scratch_shapes=_diff_scratch(c + t),
        compiler_params=_cparams(3),
        name="diff_lat",
    )(qd, kd_c, kd, vdt_c, vdt, lam_p, g_col)


def _diff_ctx(qd_c, kd_c, vdt_c, lam_p, g_col, lam_init):
    b, _, c, _ = qd_c.shape
    one_head = lambda shape: pl.BlockSpec((1, 1) + shape, lambda bi, h: (bi, h, 0, 0))
    return pl.pallas_call(
        functools.partial(_diff_ctx_kernel, lam_init=lam_init),
        grid=(b, DIFF_HEADS),
        in_specs=[one_head((c, LANES)), one_head((c, LANES)), one_head((DIFF_V_DIM, c)),
                  pl.BlockSpec(lam_p.shape, lambda bi, h: (0, 0)),
                  pl.BlockSpec(g_col.shape, lambda bi, h: (0, 0))],
        out_specs=one_head((DIFF_V_DIM, c)),
        out_shape=jax.ShapeDtypeStruct((b, DIFF_HEADS, DIFF_V_DIM, c), F32),
        scratch_shapes=_diff_scratch(c),
        compiler_params=_cparams(2),
        name="diff_ctx",
    )(qd_c, kd_c, vdt_c, lam_p, g_col)


CONV_PAD = 16
CONV_TILE = 256
SUBLANES = 8


def _conv_kernel(u_ref, w_ref, b_ref, g_ref, beta_ref, o_ref, sh_ref):
    s = u_ref.shape[1]
    zeros = jnp.zeros((CONV_PAD, CONV_WIDTH), F32)
    sh_ref[0, 0:CONV_PAD, :] = zeros
    sh_ref[0, CONV_PAD + s:CONV_PAD + s + CONV_PAD, :] = zeros
    sh_ref[0, CONV_PAD:CONV_PAD + s, :] = u_ref[0]
    first = CONV_PAD - CONV_TAPS // 2
    n_rows = s + 2 * CONV_PAD - SUBLANES
    for r in range(1, SUBLANES):
        sh_ref[r, 0:n_rows, :] = sh_ref[0, r:r + n_rows, :]
    for t0 in range(0, s, CONV_TILE):
        acc = jnp.broadcast_to(b_ref[...], (CONV_TILE, CONV_WIDTH))
        for j in range(CONV_TAPS):
            r = (first + j) % SUBLANES
            base = t0 + first + j - r
            acc = acc + sh_ref[r, base:base + CONV_TILE, :] * w_ref[j:j + 1, :]
        mu = jnp.mean(acc, axis=-1, keepdims=True)
        d = acc - mu
        var = jnp.mean(d * d, axis=-1, keepdims=True)
        y = d * lax.rsqrt(var + EPS) * g_ref[...] + beta_ref[...]
        o_ref[0, t0:t0 + CONV_TILE, :] = _silu(y)


def _conv(u, conv_w, conv_b, ln_g, ln_b):
    b, s, _ = u.shape
    row = lambda a: a.reshape(1, CONV_WIDTH)
    vec_spec = pl.BlockSpec((1, CONV_WIDTH), lambda bi: (0, 0))
    return pl.pallas_call(
        _conv_kernel,
        grid=(b,),
        in_specs=[pl.BlockSpec((1, s, CONV_WIDTH), lambda bi: (bi, 0, 0)),
                  pl.BlockSpec((CONV_TAPS, CONV_WIDTH), lambda bi: (0, 0)),
                  vec_spec, vec_spec, vec_spec],
        out_specs=pl.BlockSpec((1, s, CONV_WIDTH), lambda bi: (bi, 0, 0)),
        out_shape=jax.ShapeDtypeStruct((b, s, CONV_WIDTH), F32),
        scratch_shapes=[pltpu.VMEM((SUBLANES, s + 2 * CONV_PAD, CONV_WIDTH), F32)],
        compiler_params=_cparams(1),
        name="conformer_conv",
    )(u, conv_w, row(conv_b), row(ln_g), row(ln_b))


def _outproj_kernel(ont_ref, odt_ref, cv_ref, gate_ref, x_ref, w_ref, gpost_ref, gmod_ref, o_ref):
    tm = x_ref.shape[1]
    a = ont_ref[0].reshape(NA_WIDTH, tm).T
    bq = odt_ref[0].reshape(DIFF_WIDTH, tm).T
    mixed = (jnp.concatenate([a, bq, cv_ref[0]], axis=1) * gate_ref[0]).astype(BF16)
    y = _dot(mixed, w_ref[...])
    ms = jnp.mean(y * y, axis=-1, keepdims=True)
    o_ref[0] = x_ref[0] + gmod_ref[0] * (y * lax.rsqrt(ms + EPS) * gpost_ref[...])


def _outproj(ont, odt, cv, gate, x, w, g_post, gmod, tm):
    b, s, _ = x.shape
    gm_idx = (lambda bi, i: (bi, 0, 0)) if gmod.shape[0] == b else (lambda bi, i: (0, 0, 0))
    return pl.pallas_call(
        _outproj_kernel,
        grid=(b, s // tm),
        in_specs=[pl.BlockSpec((1, NA_PAIRS, LANES, tm), lambda bi, i: (bi, 0, 0, i)),
                  pl.BlockSpec((1, DIFF_HEADS, DIFF_V_DIM, tm), lambda bi, i: (bi, 0, 0, i)),
                  pl.BlockSpec((1, tm, CONV_WIDTH), lambda bi, i: (bi, i, 0)),
                  pl.BlockSpec((1, tm, D_MODEL), lambda bi, i: (bi, i, 0)),
                  pl.BlockSpec((1, tm, D_MODEL), lambda bi, i: (bi, i, 0)),
                  pl.BlockSpec((D_MODEL, D_MODEL), lambda bi, i: (0, 0)),
                  pl.BlockSpec((1, D_MODEL), lambda bi, i: (0, 0)),
                  pl.BlockSpec((1, 1, D_MODEL), gm_idx)],
        out_specs=pl.BlockSpec((1, tm, D_MODEL), lambda bi, i: (bi, i, 0)),
        out_shape=jax.ShapeDtypeStruct((b, s, D_MODEL), F32),
        compiler_params=_cparams(2),
        name="outproj",
    )(ont, odt, cv, gate, x, w, g_post.reshape(1, D_MODEL), gmod)


def _relayout_w_in(w_in):
    edges = np.cumsum((0, NA_WIDTH, NA_WIDTH, DIFF_WIDTH, DIFF_WIDTH, NA_WIDTH, NA_WIDTH, DIFF_WIDTH,
                       DIFF_WIDTH, CONV_WIDTH, CONV_WIDTH, CONV_WIDTH))
    na_k, na_v, df_k, df_v, na_q, na_g, df_q, df_g, cv_a, cv_b, cv_g = [
        w_in[..., a:b] for a, b in zip(edges[:-1], edges[1:])]

    def pad_heads(w):
        w = w.reshape(w.shape[:-1] + (DIFF_HEADS, DIFF_V_DIM))
        w = jnp.pad(w, ((0, 0),) * (w.ndim - 1) + ((0, LANES - DIFF_V_DIM),))
        return w.reshape(w.shape[:-2] + (DIFF_HEADS * LANES,))

    return jnp.concatenate([na_k, pad_heads(df_k), na_q, pad_heads(df_q), na_v, df_v,
                            na_g, df_g, cv_g, cv_a, cv_b], axis=-1).astype(BF16)


IN_TM = 512
OUT_TM = 512
DIFF_QB = 1024


def kernel(x, c, ctx, c_ctx, w_mod, b_mod, g_pre, g_post, w_in, w_out, na_rpb, diff_lq1, diff_lk1,
           diff_lq2, diff_lk2, diff_subln_g, conv_w, conv_b, conv_ln_g, conv_ln_b):
    batch, seq, _ = x.shape
    pad_rows = (-(batch + 1)) % 8
    cc = jnp.concatenate([c, c_ctx[None, :], jnp.zeros((pad_rows, D_MODEL), F32)], axis=0)
    mod = _modulation(cc, w_mod, b_mod)
    bias = _na_bias_tables(na_rpb.reshape(-1))
    w_in_b = _relayout_w_in(w_in)
    w_out_b = w_out.astype(BF16)
    tables = _rope_tables(seq)
    lam_p = jnp.stack([diff_lq1, diff_lk1, diff_lq2, diff_lk2], axis=1)

    xc = ctx
    for l in range(DEPTH):
        update_ctx = l < DEPTH - 1
        lam_init = 0.8 - 0.6 * math.exp(-0.3 * l)
        ss_lat = mod[l, :batch, :2 * D_MODEL].reshape(batch, 1, 1, 2 * D_MODEL)
        ss_ctx = mod[l, batch, :2 * D_MODEL].reshape(1, 1, 1, 2 * D_MODEL)
        g_col = diff_subln_g[l].reshape(DIFF_V_DIM, 1)

        kn, kd, qn, qd, vnt, vdt, gate, u = _inproj(x, ss_lat, g_pre[l], w_in_b[l], tables, IN_TM)
        kn_c, kd_c, qn_c, qd_c, vnt_c, vdt_c, gate_c, u_c = _inproj(
            xc, ss_ctx, g_pre[l], w_in_b[l], None, xc.shape[1])

        ont = _na_lat(qn, kn_c, kn, vnt_c, vnt, bias[l])
        odt = _diff_lat(qd, kd_c, kd, vdt_c, vdt, lam_p[l], g_col, lam_init, DIFF_QB)
        cv = _conv(u, conv_w[l], conv_b[l], conv_ln_g[l], conv_ln_b[l])
        gmod = mod[l, :batch, 2 * D_MODEL:].reshape(batch, 1, D_MODEL)
        x_new = _outproj(ont, odt, cv, gate, x, w_out_b[l], g_post[l], gmod, OUT_TM)

        if update_ctx:
            ont_c = _na_ctx(qn_c, kn_c, vnt_c)
            odt_c = _diff_ctx(qd_c, kd_c, vdt_c, lam_p[l], g_col, lam_init)
            cv_c = _conv(u_c, conv_w[l], conv_b[l], conv_ln_g[l], conv_ln_b[l])
            gmod_c = mod[l, batch, 2 * D_MODEL:].reshape(1, 1, D_MODEL)
            xc = _outproj(ont_c, odt_c, cv_c, gate_c, xc, w_out_b[l], g_post[l], gmod_c, xc.shape[1])
        x = x_new
    return x
```

```python
import functools
import math

import numpy as np
import jax
import jax.numpy as jnp
from jax import lax
from jax.experimental import pallas as pl
from jax.experimental.pallas import tpu as pltpu

F32 = jnp.float32
BF16 = jnp.bfloat16

D_MODEL = 1024
DEPTH = 4
GRID_W = 64
NA_HEADS = 6
NA_HEAD_DIM = 64
NA_WIDTH = NA_HEADS * NA_HEAD_DIM
NA_WIN_H = 8
NA_WIN_W = 16
DIFF_HEADS = 4
DIFF_QK_DIM = 48
DIFF_V_DIM = 2 * DIFF_QK_DIM
DIFF_WIDTH = DIFF_HEADS * DIFF_V_DIM
CONV_WIDTH = D_MODEL - NA_WIDTH - DIFF_WIDTH
CONV_TAPS = 31
ROPE_BASE = 10000.0
EPS = 1e-6
NEG_INF = -1e30
LOG2_E = math.log2(math.e)
DIFF_Q_SCALE = DIFF_QK_DIM ** -0.5 * LOG2_E
NA_Q_SCALE = NA_HEAD_DIM ** -0.5 * LOG2_E

LANES = 128
NA_PAIRS = NA_HEADS // 2
KEY_BLK = 256
NA_QBLK = 256
NA_ROWS_PER_QBLK = NA_QBLK // GRID_W
NA_LOC_BLKS = 3
NA_PATTERNS = 3
VMEM_LIMIT = 56 * 1024 * 1024

C_KN = 0
C_KD = C_KN + NA_WIDTH
C_QN = C_KD + DIFF_HEADS * LANES
C_QD = C_QN + NA_WIDTH
C_VN = C_QD + DIFF_HEADS * LANES
C_VD = C_VN + NA_WIDTH
C_GATE = C_VD + DIFF_WIDTH
C_CA = C_GATE + D_MODEL
C_CB = C_CA + CONV_WIDTH
IN_COLS_PADDED = C_CB + CONV_WIDTH


def _cparams(n_axes, vmem=VMEM_LIMIT):
    return pltpu.CompilerParams(dimension_semantics=("arbitrary",) * n_axes, vmem_limit_bytes=vmem)


def _silu(z):
    return z * jax.nn.sigmoid(z)


def _dot(a, b):
    return jnp.dot(a, b, preferred_element_type=F32)


def _dot_nt(a, b):
    return lax.dot_general(a, b, (((1,), (1,)), ((), ())), preferred_element_type=F32)


PV_LAG = 2


def _mod_kernel(cc_ref, w_ref, b_ref, o_ref):
    s = _silu(cc_ref[...])
    o_ref[0] = _dot(s, w_ref[0]) + b_ref[0]


def _modulation(cc, w_mod, b_mod):
    n_rows = cc.shape[0]
    return pl.pallas_call(
        _mod_kernel,
        grid=(DEPTH, 3),
        in_specs=[
            pl.BlockSpec((n_rows, D_MODEL), lambda l, j: (0, 0)),
            pl.BlockSpec((1, D_MODEL, D_MODEL), lambda l, j: (l, 0, j)),
            pl.BlockSpec((1, 1, D_MODEL), lambda l, j: (l, 0, j)),
        ],
        out_specs=pl.BlockSpec((1, n_rows, D_MODEL), lambda l, j: (l, 0, j)),
        out_shape=jax.ShapeDtypeStruct((DEPTH, n_rows, 3 * D_MODEL), F32),
        compiler_params=_cparams(2),
        name="adaln_mod",
    )(cc, w_mod, b_mod.reshape(DEPTH, 1, 3 * D_MODEL))


def _na_bias_kernel(rpb_ref, o_ref):
    l = pl.program_id(0)
    h = pl.program_id(1)
    n_dr = 2 * NA_WIN_H - 1
    n_dc = 2 * NA_WIN_W - 1
    base = (l * NA_HEADS + h) * (n_dr * n_dc)
    kc = lax.broadcasted_iota(jnp.int32, (GRID_W, LANES), 0)
    lane = lax.broadcasted_iota(jnp.int32, (GRID_W, LANES), 1)
    qc = lane % GRID_W
    cstart = jnp.clip(qc - NA_WIN_W // 2, 0, GRID_W - NA_WIN_W)
    col_ok = (kc >= cstart) & (kc < cstart + NA_WIN_W)
    dc = kc - qc + (NA_WIN_W - 1)
    neg = jnp.full((GRID_W, LANES), NEG_INF, F32)
    dc_masks = [col_ok & (dc == i) for i in range(n_dc)]
    tiles = []
    for dr in range(n_dr):
        t = neg
        for i in range(n_dc):
            t = jnp.where(dc_masks[i], rpb_ref[base + dr * n_dc + i], t)
        tiles.append(t * LOG2_E)
    left = lane < GRID_W

    def tile(dr, ok):
        return tiles[dr] if ok else neg

    patterns = (
        (lambda i: (0, NA_WIN_H), NA_WIN_H - 1),
        (lambda i: (i, i + NA_WIN_H), NA_WIN_H - 1 - NA_ROWS_PER_QBLK),
        (lambda i: (4, 4 + NA_WIN_H), -1),
    )
    for p, (rng, off) in enumerate(patterns):
        for j in range(NA_LOC_BLKS * KEY_BLK // GRID_W):
            halves = []
            for i0 in (0, 2):
                pair = []
                for i in (i0, i0 + 1):
                    lo, hi = rng(i)
                    pair.append(tile(j - i + off, lo <= j < hi) if 0 <= j - i + off < n_dr else neg)
                halves.append(jnp.where(left, pair[0], pair[1]))
            o_ref[0, 0, p, j * GRID_W:(j + 1) * GRID_W, :] = jnp.concatenate(halves, axis=1)


def _na_bias_tables(rpb_flat):
    n_keys = NA_LOC_BLKS * KEY_BLK
    return pl.pallas_call(
        _na_bias_kernel,
        grid=(DEPTH, NA_HEADS),
        in_specs=[pl.BlockSpec(memory_space=pltpu.SMEM)],
        out_specs=pl.BlockSpec((1, 1, NA_PATTERNS, n_keys, NA_QBLK), lambda l, h: (l, h, 0, 0, 0)),
        out_shape=jax.ShapeDtypeStruct((DEPTH, NA_HEADS, NA_PATTERNS, n_keys, NA_QBLK), F32),
        compiler_params=_cparams(2),
        name="na_bias_tables",
    )(rpb_flat)


def _rope_tables(seq):
    half = DIFF_QK_DIM // 2
    t = np.arange(seq)
    pos = np.stack([t // GRID_W, t % GRID_W], axis=1).astype(np.float64)
    lane = np.arange(LANES)
    live = lane < DIFF_V_DIM
    seg = np.where(live, lane // half, 0)
    i = lane % half
    inv = ROPE_BASE ** (-(2.0 * (i % (half // 2))) / half)
    ang = pos[:, seg % 2] * inv[None, :]
    cos = np.where(live[None, :], np.cos(ang), 1.0)
    sgn = np.where(i < half // 2, -1.0, 1.0)
    sin = np.where(live[None, :], np.sin(ang) * sgn[None, :], 0.0)
    return jnp.asarray(cos, F32), jnp.asarray(sin, F32)


def _inproj_kernel(*refs, rope):
    if rope:
        x_ref, ss_ref, gpre_ref, w_ref, cos_ref, sin_ref = refs[:6]
        outs = refs[6:]
    else:
        x_ref, ss_ref, gpre_ref, w_ref = refs[:4]
        outs = refs[4:]
    kn_ref, kd_ref, qn_ref, qd_ref, vnt_ref, vdt_ref, gate_ref, u_ref = outs

    x = x_ref[0]
    ms = jnp.mean(x * x, axis=-1, keepdims=True)
    hn = x * lax.rsqrt(ms + EPS) * gpre_ref[...]
    ss = ss_ref[0, 0]
    hb = (hn * (1.0 + ss[:, D_MODEL:]) + ss[:, :D_MODEL]).astype(BF16)

    def proj(a, b):
        return _dot(hb, w_ref[0, :, a:b])

    if rope:
        cos = cos_ref[...]
        sin = sin_ref[...]
        lane = lax.broadcasted_iota(jnp.int32, cos.shape, 1)
        quarter = DIFF_QK_DIM // 4
        first = (lane % (2 * quarter)) < quarter

        def rot(zb):
            up = pltpu.roll(zb, LANES - quarter, axis=1)
            dn = pltpu.roll(zb, quarter, axis=1)
            return zb * cos + jnp.where(first, up, dn) * sin
    else:
        def rot(zb):
            return zb

    def store_kn(z):
        for p in range(NA_PAIRS):
            kn_ref[0, p] = z[:, p * LANES:(p + 1) * LANES].astype(BF16)

    def store_kd(z):
        for h in range(DIFF_HEADS):
            kd_ref[0, h] = rot(z[:, h * LANES:(h + 1) * LANES]).astype(BF16)

    def store_qn(z):
        for p in range(NA_PAIRS):
            qn_ref[0, p] = (z[:, p * LANES:(p + 1) * LANES] * NA_Q_SCALE).astype(BF16)

    def store_qd(z):
        for h in range(DIFF_HEADS):
            qd_ref[0, h] = (rot(z[:, h * LANES:(h + 1) * LANES]) * DIFF_Q_SCALE).astype(BF16)

    def store_vn(z):
        zt = z.T
        for p in range(NA_PAIRS):
            vnt_ref[0, p] = zt[p * LANES:(p + 1) * LANES].astype(BF16)

    def store_vd(z):
        zt = z.T
        for h in range(DIFF_HEADS):
            vdt_ref[0, h] = zt[h * DIFF_V_DIM:(h + 1) * DIFF_V_DIM].astype(BF16)

    def store_gate(z):
        gate_ref[0] = _silu(z)

    def store_u(ab):
        u_ref[0] = ab[0] * jax.nn.sigmoid(ab[1])

    z = proj(C_KN, C_VN)
    store_kn(z[:, C_KN:C_KD])
    store_kd(z[:, C_KD:C_QN])
    store_qn(z[:, C_QN:C_QD])
    store_qd(z[:, C_QD:C_VN])
    z = proj(C_VN, C_GATE)
    store_vn(z[:, :NA_WIDTH])
    store_vd(z[:, NA_WIDTH:])
    store_gate(proj(C_GATE, C_CA))
    z = proj(C_CA, IN_COLS_PADDED)
    store_u((z[:, :CONV_WIDTH], z[:, CONV_WIDTH:]))


def _inproj(x, ss, g_pre, w, layer, tables, tm):
    b, s, _ = x.shape
    rope = tables is not None
    ss_idx = (lambda bi, i: (bi, 0, 0, 0)) if ss.shape[0] == b else (lambda bi, i: (0, 0, 0, 0))
    in_specs = [
        pl.BlockSpec((1, tm, D_MODEL), lambda bi, i: (bi, i, 0)),
        pl.BlockSpec((1, 1, 1, 2 * D_MODEL), ss_idx),
        pl.BlockSpec((1, D_MODEL), lambda bi, i: (0, 0)),
        pl.BlockSpec((1, D_MODEL, IN_COLS_PADDED), lambda bi, i: (layer, 0, 0)),
    ]
    args = [x, ss, g_pre.reshape(1, D_MODEL), w]
    if rope:
        in_specs += [pl.BlockSpec((tm, LANES), lambda bi, i: (i, 0))] * 2
        args += list(tables)
    row_major = lambda n: pl.BlockSpec((1, n, tm, LANES), lambda bi, i: (bi, 0, i, 0))
    out_specs = [
        row_major(NA_PAIRS), row_major(DIFF_HEADS), row_major(NA_PAIRS), row_major(DIFF_HEADS),
        pl.BlockSpec((1, NA_PAIRS, LANES, tm), lambda bi, i: (bi, 0, 0, i)),
        pl.BlockSpec((1, DIFF_HEADS, DIFF_V_DIM, tm), lambda bi, i: (bi, 0, 0, i)),
        pl.BlockSpec((1, tm, D_MODEL), lambda bi, i: (bi, i, 0)),
        pl.BlockSpec((1, tm, CONV_WIDTH), lambda bi, i: (bi, i, 0)),
    ]
    out_shape = [
        jax.ShapeDtypeStruct((b, NA_PAIRS, s, LANES), BF16),
        jax.ShapeDtypeStruct((b, DIFF_HEADS, s, LANES), BF16),
        jax.ShapeDtypeStruct((b, NA_PAIRS, s, LANES), BF16),
        jax.ShapeDtypeStruct((b, DIFF_HEADS, s, LANES), BF16),
        jax.ShapeDtypeStruct((b, NA_PAIRS, LANES, s), BF16),
        jax.ShapeDtypeStruct((b, DIFF_HEADS, DIFF_V_DIM, s), BF16),
        jax.ShapeDtypeStruct((b, s, D_MODEL), F32),
        jax.ShapeDtypeStruct((b, s, CONV_WIDTH), F32),
    ]
    return pl.pallas_call(
        functools.partial(_inproj_kernel, rope=rope),
        grid=(b, s // tm),
        in_specs=in_specs,
        out_specs=out_specs,
        out_shape=out_shape,
        compiler_params=_cparams(2),
        name="inproj_lat" if rope else "inproj_ctx",
    )(*args)


def _na_heads(q_ref, k_refs, vt_refs, bias_ref, o_ref, s_ref):
    lane = lax.broadcasted_iota(jnp.int32, (NA_QBLK, LANES), 1)
    n_blk = len(k_refs)

    def masked_queries(head):
        q = q_ref[0, head // 2]
        lo = (head % 2) * NA_HEAD_DIM
        return jnp.where((lane >= lo) & (lane < lo + NA_HEAD_DIM), q, jnp.zeros_like(q))

    def score_block(head, c, qm, m):
        rows = slice(c * KEY_BLK, (c + 1) * KEY_BLK)
        s = _dot_nt(k_refs[c][0, head // 2], qm)
        if c > 0:
            s = s + bias_ref[0, head, 0, (c - 1) * KEY_BLK:c * KEY_BLK, :]
        s_ref[head % 2, rows, :] = s
        mc = jnp.max(s, axis=0, keepdims=True)
        m[0] = mc if m[0] is None else jnp.maximum(m[0], mc)

    def exp_block(head, c, m, denom):
        e = jnp.exp2(s_ref[head % 2, c * KEY_BLK:(c + 1) * KEY_BLK, :] - m[0])
        dc = jnp.sum(e, axis=0, keepdims=True)
        denom[0] = dc if denom[0] is None else denom[0] + dc
        return e.astype(BF16)

    def pv_block(head, c, e, acc):
        rows = slice((head % 2) * NA_HEAD_DIM, (head % 2 + 1) * NA_HEAD_DIM)
        oc = _dot(vt_refs[c][0, head // 2, rows, :], e)
        acc[0] = oc if acc[0] is None else acc[0] + oc

    m_next = [None]
    qm = masked_queries(0)
    for c in range(n_blk):
        score_block(0, c, qm, m_next)
    for head in range(NA_HEADS):
        m, m_next = m_next, [None]
        denom, acc = [None], [None]
        if head + 1 < NA_HEADS:
            qm = masked_queries(head + 1)
        waiting = []
        for c in range(n_blk):
            if head + 1 < NA_HEADS:
                score_block(head + 1, c, qm, m_next)
            waiting.append((c, exp_block(head, c, m, denom)))
            if len(waiting) > PV_LAG:
                pv_block(head, *waiting.pop(0), acc)
        for item in waiting:
            pv_block(head, *item, acc)
        rows = slice((head % 2) * NA_HEAD_DIM, (head % 2 + 1) * NA_HEAD_DIM)
        o_ref[0, head // 2, rows, :] = acc[0] * (1.0 / denom[0])


def _na_lat_kernel(q_ref, kc_ref, k0_ref, k1_ref, k2_ref, vc_ref, v0_ref, v1_ref, v2_ref, bias_ref, o_ref, s_ref):
    _na_heads(q_ref, (kc_ref, k0_ref, k1_ref, k2_ref), (vc_ref, v0_ref, v1_ref, v2_ref), bias_ref, o_ref, s_ref)


def _na_ctx_kernel(q_ref, kc_ref, vc_ref, o_ref, s_ref):
    _na_heads(q_ref, (kc_ref,), (vc_ref,), None, o_ref, s_ref)


def _na_lat(qn, kn_c, kn, vnt_c, vnt, bias, layer):
    b, _, t, _ = qn.shape
    n_qblk = t // NA_QBLK
    last_start = t // KEY_BLK - NA_LOC_BLKS

    def start(j):
        return jnp.clip(j - 1, 0, last_start)

    def pattern(j):
        return jnp.where(j == 0, 0, jnp.where(j == n_qblk - 1, 2, 1))

    row_blk = (1, NA_PAIRS, KEY_BLK, LANES)
    col_blk = (1, NA_PAIRS, LANES, KEY_BLK)
    k_specs = [pl.BlockSpec(row_blk, functools.partial(lambda bi, j, i: (bi, 0, start(j) + i, 0), i=i))
               for i in range(NA_LOC_BLKS)]
    v_specs = [pl.BlockSpec(col_blk, functools.partial(lambda bi, j, i: (bi, 0, 0, start(j) + i), i=i))
               for i in range(NA_LOC_BLKS)]
    return pl.pallas_call(
        _na_lat_kernel,
        grid=(b, n_qblk),
        in_specs=[pl.BlockSpec(row_blk, lambda bi, j: (bi, 0, j, 0)),
                  pl.BlockSpec(row_blk, lambda bi, j: (bi, 0, 0, 0))] + k_specs
                 + [pl.BlockSpec(col_blk, lambda bi, j: (bi, 0, 0, 0))] + v_specs
                 + [pl.BlockSpec((1, NA_HEADS, 1, NA_LOC_BLKS * KEY_BLK, NA_QBLK),
                                 lambda bi, j: (layer, 0, pattern(j), 0, 0))],
        out_specs=pl.BlockSpec(col_blk, lambda bi, j: (bi, 0, 0, j)),
        out_shape=jax.ShapeDtypeStruct((b, NA_PAIRS, LANES, t), F32),
        scratch_shapes=[pltpu.VMEM((2, (1 + NA_LOC_BLKS) * KEY_BLK, NA_QBLK), F32)],
        compiler_params=_cparams(2),
        name="na_lat",
    )(qn, kn_c, kn, kn, kn, vnt_c, vnt, vnt, vnt, bias)


def _na_ctx(qn_c, kn_c, vnt_c):
    b, _, c, _ = qn_c.shape
    row_blk = (1, NA_PAIRS, c, LANES)
    col_blk = (1, NA_PAIRS, LANES, c)
    return pl.pallas_call(
        _na_ctx_kernel,
        grid=(b,),
        in_specs=[pl.BlockSpec(row_blk, lambda bi: (bi, 0, 0, 0)),
                  pl.BlockSpec(row_blk, lambda bi: (bi, 0, 0, 0)),
                  pl.BlockSpec(col_blk, lambda bi: (bi, 0, 0, 0))],
        out_specs=pl.BlockSpec(col_blk, lambda bi: (bi, 0, 0, 0)),
        out_shape=jax.ShapeDtypeStruct((b, NA_PAIRS, LANES, c), F32),
        scratch_shapes=[pltpu.VMEM((2, c, c), F32)],
        compiler_params=_cparams(1),
        name="na_ctx",
    )(qn_c, kn_c, vnt_c)


DIFF_SUB = 256


def _diff_body(q_ref, key_blocks, val_blocks, lam_ref, g_ref, o_ref, s_ref, lam_init):
    n_sub = q_ref.shape[2] // DIFF_SUB
    lp = lam_ref[...]
    lam = (jnp.exp(jnp.sum(lp[0:1] * lp[1:2], axis=1, keepdims=True))
           - jnp.exp(jnp.sum(lp[2:3] * lp[3:4], axis=1, keepdims=True)) + lam_init)
    gain = g_ref[...] * (1.0 - lam_init)

    def masked_queries(i):
        q = q_ref[0, 0, i * DIFF_SUB:(i + 1) * DIFF_SUB, :]
        lane = lax.broadcasted_iota(jnp.int32, q.shape, 1)
        zero = jnp.zeros_like(q)
        return (jnp.where(lane < DIFF_QK_DIM, q, zero),
                jnp.where((lane >= DIFF_QK_DIM) & (lane < 2 * DIFF_QK_DIM), q, zero))

    def score_block(i, c, qm, m):
        k = key_blocks[c]()
        for mp in range(2):
            s = _dot_nt(k, qm[mp])
            s_ref[i % 2, mp, c * KEY_BLK:(c + 1) * KEY_BLK, :] = s
            mc = jnp.max(s, axis=0, keepdims=True)
            m[mp] = mc if m[mp] is None else jnp.maximum(m[mp], mc)

    def exp_block(i, c, m, l):
        ps = []
        for mp in range(2):
            p = jnp.exp2(s_ref[i % 2, mp, c * KEY_BLK:(c + 1) * KEY_BLK, :] - m[mp])
            lc = jnp.sum(p, axis=0, keepdims=True)
            l[mp] = lc if l[mp] is None else l[mp] + lc
            ps.append(p.astype(BF16))
        return ps

    def pv_block(c, ps, acc):
        vt = val_blocks[c]()
        for mp in range(2):
            pv = _dot(vt, ps[mp])
            acc[mp] = pv if acc[mp] is None else acc[mp] + pv

    n_blk = len(key_blocks)
    m_next = [None, None]
    qm = masked_queries(0)
    for c in range(n_blk):
        score_block(0, c, qm, m_next)
    for i in range(n_sub):
        m, m_next = m_next, [None, None]
        l, acc = [None, None], [None, None]
        if i + 1 < n_sub:
            qm = masked_queries(i + 1)
        waiting = []
        for c in range(n_blk):
            if i + 1 < n_sub:
                score_block(i + 1, c, qm, m_next)
            waiting.append((c, exp_block(i, c, m, l)))
            if len(waiting) > PV_LAG:
                pv_block(*waiting.pop(0), acc)
        for item in waiting:
            pv_block(*item, acc)
        out = acc[0] * (1.0 / l[0]) - acc[1] * (lam / l[1])
        ms = jnp.mean(out * out, axis=0, keepdims=True)
        o_ref[0, 0, :, i * DIFF_SUB:(i + 1) * DIFF_SUB] = out * lax.rsqrt(ms + EPS) * gain


def _diff_lat_kernel(q_ref, kc_ref, k_ref, vc_ref, v_ref, lam_ref, g_ref, o_ref, s_ref, *, lam_init):
    n_lat = k_ref.shape[2] // KEY_BLK
    keys = [lambda: kc_ref[0, 0]] + [
        functools.partial(lambda c: k_ref[0, 0, c * KEY_BLK:(c + 1) * KEY_BLK, :], c) for c in range(n_lat)]
    vals = [lambda: vc_ref[0, 0]] + [
        functools.partial(lambda c: v_ref[0, 0, :, c * KEY_BLK:(c + 1) * KEY_BLK], c) for c in range(n_lat)]
    _diff_body(q_ref, keys, vals, lam_ref, g_ref, o_ref, s_ref, lam_init)


def _diff_ctx_kernel(q_ref, kc_ref, vc_ref, lam_ref, g_ref, o_ref, s_ref, *, lam_init):
    _diff_body(q_ref, [lambda: kc_ref[0, 0]], [lambda: vc_ref[0, 0]], lam_ref, g_ref, o_ref, s_ref, lam_init)


def _diff_scratch(n_keys):
    return [pltpu.VMEM((2, 2, n_keys, DIFF_SUB), F32)]


def _diff_lat(qd, kd_c, kd, vdt_c, vdt, lam_p, g_col, lam_init, qb):
    b, _, t, _ = qd.shape
    c = kd_c.shape[2]
    one_head = lambda shape, idx: pl.BlockSpec((1, 1) + shape, idx)
    return pl.pallas_call(
        functools.partial(_diff_lat_kernel, lam_init=lam_init),
        grid=(b, DIFF_HEADS, t // qb),
        in_specs=[one_head((qb, LANES), lambda bi, h, j: (bi, h, j, 0)),
                  one_head((c, LANES), lambda bi, h, j: (bi, h, 0, 0)),
                  one_head((t, LANES), lambda bi, h, j: (bi, h, 0, 0)),
                  one_head((DIFF_V_DIM, c), lambda bi, h, j: (bi, h, 0, 0)),
                  one_head((DIFF_V_DIM, t), lambda bi, h, j: (bi, h, 0, 0)),
                  pl.BlockSpec(lam_p.shape, lambda bi, h, j: (0, 0)),
                  pl.BlockSpec(g_col.shape, lambda bi, h, j: (0, 0))],
        out_specs=one_head((DIFF_V_DIM, qb), lambda bi, h, j: (bi, h, 0, j)),
        out_shape=jax.ShapeDtypeStruct((b, DIFF_HEADS, DIFF_V_DIM, t), F32),
        scratch_shapes=_diff_scratch(c + t),
        compiler_params=_cparams(3),
        name="diff_lat",
    )(qd, kd_c, kd, vdt_c, vdt, lam_p, g_col)


def _diff_ctx(qd_c, kd_c, vdt_c, lam_p, g_col, lam_init):
    b, _, c, _ = qd_c.shape
    one_head = lambda shape: pl.BlockSpec((1, 1) + shape, lambda bi, h: (bi, h, 0, 0))
    return pl.pallas_call(
        functools.partial(_diff_ctx_kernel, lam_init=lam_init),
        grid=(b, DIFF_HEADS),
        in_specs=[one_head((c, LANES)), one_head((c, LANES)), one_head((DIFF_V_DIM, c)),
                  pl.BlockSpec(lam_p.shape, lambda bi, h: (0, 0)),
                  pl.BlockSpec(g_col.shape, lambda bi, h: (0, 0))],
        out_specs=one_head((DIFF_V_DIM, c)),
        out_shape=jax.ShapeDtypeStruct((b, DIFF_HEADS, DIFF_V_DIM, c), F32),
        scratch_shapes=_diff_scratch(c),
        compiler_params=_cparams(2),
        name="diff_ctx",
    )(qd_c, kd_c, vdt_c, lam_p, g_col)


CONV_PAD = 16
CONV_TILE = 256
SUBLANES = 8


def _conv_kernel(u_ref, w_ref, b_ref, g_ref, beta_ref, o_ref, sh_ref):
    s = u_ref.shape[1]
    zeros = jnp.zeros((CONV_PAD, CONV_WIDTH), F32)
    sh_ref[0, 0:CONV_PAD, :] = zeros
    sh_ref[0, CONV_PAD + s:CONV_PAD + s + CONV_PAD, :] = zeros
    sh_ref[0, CONV_PAD:CONV_PAD + s, :] = u_ref[0]
    first = CONV_PAD - CONV_TAPS // 2
    n_rows = s + 2 * CONV_PAD - SUBLANES
    for r in range(1, SUBLANES):
        sh_ref[r, 0:n_rows, :] = sh_ref[0, r:r + n_rows, :]
    for t0 in range(0, s, CONV_TILE):
        acc = jnp.broadcast_to(b_ref[...], (CONV_TILE, CONV_WIDTH))
        for j in range(CONV_TAPS):
            r = (first + j) % SUBLANES
            base = t0 + first + j - r
            acc = acc + sh_ref[r, base:base + CONV_TILE, :] * w_ref[j:j + 1, :]
        mu = jnp.mean(acc, axis=-1, keepdims=True)
        d = acc - mu
        var = jnp.mean(d * d, axis=-1, keepdims=True)
        y = d * lax.rsqrt(var + EPS) * g_ref[...] + beta_ref[...]
        o_ref[0, t0:t0 + CONV_TILE, :] = _silu(y)


def _conv(u, conv_w, conv_b, ln_g, ln_b):
    b, s, _ = u.shape
    row = lambda a: a.reshape(1, CONV_WIDTH)
    vec_spec = pl.BlockSpec((1, CONV_WIDTH), lambda bi: (0, 0))
    return pl.pallas_call(
        _conv_kernel,
        grid=(b,),
        in_specs=[pl.BlockSpec((1, s, CONV_WIDTH), lambda bi: (bi, 0, 0)),
                  pl.BlockSpec((CONV_TAPS, CONV_WIDTH), lambda bi: (0, 0)),
                  vec_spec, vec_spec, vec_spec],
        out_specs=pl.BlockSpec((1, s, CONV_WIDTH), lambda bi: (bi, 0, 0)),
        out_shape=jax.ShapeDtypeStruct((b, s, CONV_WIDTH), F32),
        scratch_shapes=[pltpu.VMEM((SUBLANES, s + 2 * CONV_PAD, CONV_WIDTH), F32)],
        compiler_params=_cparams(1),
        name="conformer_conv",
    )(u, conv_w, row(conv_b), row(ln_g), row(ln_b))


def _outproj_kernel(ont_ref, odt_ref, cv_ref, gate_ref, x_ref, w_ref, gpost_ref, gmod_ref, o_ref):
    tm = x_ref.shape[1]
    a = ont_ref[0].reshape(NA_WIDTH, tm).T
    bq = odt_ref[0].reshape(DIFF_WIDTH, tm).T
    mixed = (jnp.concatenate([a, bq, cv_ref[0]], axis=1) * gate_ref[0]).astype(BF16)
    y = _dot(mixed, w_ref[0])
    ms = jnp.mean(y * y, axis=-1, keepdims=True)
    o_ref[0] = x_ref[0] + gmod_ref[0] * (y * lax.rsqrt(ms + EPS) * gpost_ref[...])


def _outproj(ont, odt, cv, gate, x, w, layer, g_post, gmod, tm):
    b, s, _ = x.shape
    gm_idx = (lambda bi, i: (bi, 0, 0)) if gmod.shape[0] == b else (lambda bi, i: (0, 0, 0))
    return pl.pallas_call(
        _outproj_kernel,
        grid=(b, s // tm),
        in_specs=[pl.BlockSpec((1, NA_PAIRS, LANES, tm), lambda bi, i: (bi, 0, 0, i)),
                  pl.BlockSpec((1, DIFF_HEADS, DIFF_V_DIM, tm), lambda bi, i: (bi, 0, 0, i)),
                  pl.BlockSpec((1, tm, CONV_WIDTH), lambda bi, i: (bi, i, 0)),
                  pl.BlockSpec((1, tm, D_MODEL), lambda bi, i: (bi, i, 0)),
                  pl.BlockSpec((1, tm, D_MODEL), lambda bi, i: (bi, i, 0)),
                  pl.BlockSpec((1, D_MODEL, D_MODEL), lambda bi, i: (layer, 0, 0)),
                  pl.BlockSpec((1, D_MODEL), lambda bi, i: (0, 0)),
                  pl.BlockSpec((1, 1, D_MODEL), gm_idx)],
        out_specs=pl.BlockSpec((1, tm, D_MODEL), lambda bi, i: (bi, i, 0)),
        out_shape=jax.ShapeDtypeStruct((b, s, D_MODEL), F32),
        compiler_params=_cparams(2),
        name="outproj",
    )(ont, odt, cv, gate, x, w, g_post.reshape(1, D_MODEL), gmod)


def _relayout_w_in(w_in):
    edges = np.cumsum((0, NA_WIDTH, NA_WIDTH, DIFF_WIDTH, DIFF_WIDTH, NA_WIDTH, NA_WIDTH, DIFF_WIDTH,
                       DIFF_WIDTH, CONV_WIDTH, CONV_WIDTH, CONV_WIDTH))
    na_k, na_v, df_k, df_v, na_q, na_g, df_q, df_g, cv_a, cv_b, cv_g = [
        w_in[..., a:b] for a, b in zip(edges[:-1], edges[1:])]

    def pad_heads(w):
        w = w.reshape(w.shape[:-1] + (DIFF_HEADS, DIFF_V_DIM))
        w = jnp.pad(w, ((0, 0),) * (w.ndim - 1) + ((0, LANES - DIFF_V_DIM),))
        return w.reshape(w.shape[:-2] + (DIFF_HEADS * LANES,))

    return jnp.concatenate([na_k, pad_heads(df_k), na_q, pad_heads(df_q), na_v, df_v,
                            na_g, df_g, cv_g, cv_a, cv_b], axis=-1).astype(BF16)


IN_TM = 512
OUT_TM = 512
DIFF_QB = 2048


def kernel(x, c, ctx, c_ctx, w_mod, b_mod, g_pre, g_post, w_in, w_out, na_rpb, diff_lq1, diff_lk1,
           diff_lq2, diff_lk2, diff_subln_g, conv_w, conv_b, conv_ln_g, conv_ln_b):
    batch, seq, _ = x.shape
    pad_rows = (-(batch + 1)) % 8
    cc = jnp.concatenate([c, c_ctx[None, :], jnp.zeros((pad_rows, D_MODEL), F32)], axis=0)
    mod = _modulation(cc, w_mod, b_mod)
    bias = _na_bias_tables(na_rpb.reshape(-1))
    w_in_b = _relayout_w_in(w_in)
    w_out_b = w_out.astype(BF16)
    tables = _rope_tables(seq)
    lam_p = jnp.stack([diff_lq1, diff_lk1, diff_lq2, diff_lk2], axis=1)

    xc = ctx
    for l in range(DEPTH):
        update_ctx = l < DEPTH - 1
        lam_init = 0.8 - 0.6 * math.exp(-0.3 * l)
        ss_lat = mod[l, :batch, :2 * D_MODEL].reshape(batch, 1, 1, 2 * D_MODEL)
        ss_ctx = mod[l, batch, :2 * D_MODEL].reshape(1, 1, 1, 2 * D_MODEL)
        g_col = diff_subln_g[l].reshape(DIFF_V_DIM, 1)

        kn, kd, qn, qd, vnt, vdt, gate, u = _inproj(x, ss_lat, g_pre[l], w_in_b, l, tables, IN_TM)
        kn_c, kd_c, qn_c, qd_c, vnt_c, vdt_c, gate_c, u_c = _inproj(
            xc, ss_ctx, g_pre[l], w_in_b, l, None, xc.shape[1])

        ont = _na_lat(qn, kn_c, kn, vnt_c, vnt, bias, l)
        odt = _diff_lat(qd, kd_c, kd, vdt_c, vdt, lam_p[l], g_col, lam_init, DIFF_QB)
        cv = _conv(u, conv_w[l], conv_b[l], conv_ln_g[l], conv_ln_b[l])
        gmod = mod[l, :batch, 2 * D_MODEL:].reshape(batch, 1, D_MODEL)
        x_new = _outproj(ont, odt, cv, gate, x, w_out_b, l, g_post[l], gmod, OUT_TM)

        if update_ctx:
            ont_c = _na_ctx(qn_c, kn_c, vnt_c)
            odt_c = _diff_ctx(qd_c, kd_c, vdt_c, lam_p[l], g_col, lam_init)
            cv_c = _conv(u_c, conv_w[l], conv_b[l], conv_ln_g[l], conv_ln_b[l])
            gmod_c = mod[l, batch, 2 * D_MODEL:].reshape(1, 1, D_MODEL)
            xc = _outproj(ont_c, odt_c, cv_c, gate_c, xc, w_out_b, l, g_post[l], gmod_c, xc.shape[1])
        x = x_new
    return x
```

```python
import functools
import math

import numpy as np
import jax
import jax.numpy as jnp
from jax import lax
from jax.experimental import pallas as pl
from jax.experimental.pallas import tpu as pltpu

F32 = jnp.float32
BF16 = jnp.bfloat16

D_MODEL = 1024
DEPTH = 4
GRID_W = 64
NA_HEADS = 6
NA_HEAD_DIM = 64
NA_WIDTH = NA_HEADS * NA_HEAD_DIM
NA_WIN_H = 8
NA_WIN_W = 16
DIFF_HEADS = 4
DIFF_QK_DIM = 48
DIFF_V_DIM = 2 * DIFF_QK_DIM
DIFF_WIDTH = DIFF_HEADS * DIFF_V_DIM
CONV_WIDTH = D_MODEL - NA_WIDTH - DIFF_WIDTH
CONV_TAPS = 31
ROPE_BASE = 10000.0
EPS = 1e-6
NEG_INF = -1e30
LOG2_E = math.log2(math.e)
DIFF_Q_SCALE = DIFF_QK_DIM ** -0.5 * LOG2_E
NA_Q_SCALE = NA_HEAD_DIM ** -0.5 * LOG2_E

LANES = 128
NA_PAIRS = NA_HEADS // 2
KEY_BLK = 256
NA_QBLK = 256
NA_ROWS_PER_QBLK = NA_QBLK // GRID_W
NA_LOC_BLKS = 3
NA_PATTERNS = 3
VMEM_LIMIT = 56 * 1024 * 1024

C_KN = 0
C_KD = C_KN + NA_WIDTH
C_QN = C_KD + DIFF_HEADS * LANES
C_QD = C_QN + NA_WIDTH
C_VN = C_QD + DIFF_HEADS * LANES
C_VD = C_VN + NA_WIDTH
C_GATE = C_VD + DIFF_WIDTH
C_CA = C_GATE + D_MODEL
C_CB = C_CA + CONV_WIDTH
IN_COLS_PADDED = C_CB + CONV_WIDTH


def _cparams(n_axes, vmem=VMEM_LIMIT):
    return pltpu.CompilerParams(dimension_semantics=("arbitrary",) * n_axes, vmem_limit_bytes=vmem)


def _silu(z):
    return z * jax.nn.sigmoid(z)


def _dot(a, b):
    return jnp.dot(a, b, preferred_element_type=F32)


def _dot_nt(a, b):
    return lax.dot_general(a, b, (((1,), (1,)), ((), ())), preferred_element_type=F32)


PV_LAG = 2


def _mod_kernel(cc_ref, w_ref, b_ref, o_ref):
    s = _silu(cc_ref[...])
    o_ref[0] = _dot(s, w_ref[0]) + b_ref[0]


def _modulation(cc, w_mod, b_mod):
    n_rows = cc.shape[0]
    return pl.pallas_call(
        _mod_kernel,
        grid=(DEPTH, 3),
        in_specs=[
            pl.BlockSpec((n_rows, D_MODEL), lambda l, j: (0, 0)),
            pl.BlockSpec((1, D_MODEL, D_MODEL), lambda l, j: (l, 0, j)),
            pl.BlockSpec((1, 1, D_MODEL), lambda l, j: (l, 0, j)),
        ],
        out_specs=pl.BlockSpec((1, n_rows, D_MODEL), lambda l, j: (l, 0, j)),
        out_shape=jax.ShapeDtypeStruct((DEPTH, n_rows, 3 * D_MODEL), F32),
        compiler_params=_cparams(2),
        name="adaln_mod",
    )(cc, w_mod, b_mod.reshape(DEPTH, 1, 3 * D_MODEL))


def _na_bias_kernel(rpb_ref, o_ref):
    l = pl.program_id(0)
    h = pl.program_id(1)
    n_dr = 2 * NA_WIN_H - 1
    n_dc = 2 * NA_WIN_W - 1
    base = (l * NA_HEADS + h) * (n_dr * n_dc)
    kc = lax.broadcasted_iota(jnp.int32, (GRID_W, LANES), 0)
    lane = lax.broadcasted_iota(jnp.int32, (GRID_W, LANES), 1)
    qc = lane % GRID_W
    cstart = jnp.clip(qc - NA_WIN_W // 2, 0, GRID_W - NA_WIN_W)
    col_ok = (kc >= cstart) & (kc < cstart + NA_WIN_W)
    dc = kc - qc + (NA_WIN_W - 1)
    neg = jnp.full((GRID_W, LANES), NEG_INF, F32)
    dc_masks = [col_ok & (dc == i) for i in range(n_dc)]
    tiles = []
    for dr in range(n_dr):
        t = neg
        for i in range(n_dc):
            t = jnp.where(dc_masks[i], rpb_ref[base + dr * n_dc + i], t)
        tiles.append(t * LOG2_E)
    left = lane < GRID_W

    def tile(dr, ok):
        return tiles[dr] if ok else neg

    patterns = (
        (lambda i: (0, NA_WIN_H), NA_WIN_H - 1),
        (lambda i: (i, i + NA_WIN_H), NA_WIN_H - 1 - NA_ROWS_PER_QBLK),
        (lambda i: (4, 4 + NA_WIN_H), -1),
    )
    for p, (rng, off) in enumerate(patterns):
        for j in range(NA_LOC_BLKS * KEY_BLK // GRID_W):
            halves = []
            for i0 in (0, 2):
                pair = []
                for i in (i0, i0 + 1):
                    lo, hi = rng(i)
                    pair.append(tile(j - i + off, lo <= j < hi) if 0 <= j - i + off < n_dr else neg)
                halves.append(jnp.where(left, pair[0], pair[1]))
            o_ref[0, 0, p, j * GRID_W:(j + 1) * GRID_W, :] = jnp.concatenate(halves, axis=1)


def _na_bias_tables(rpb_flat):
    n_keys = NA_LOC_BLKS * KEY_BLK
    return pl.pallas_call(
        _na_bias_kernel,
        grid=(DEPTH, NA_HEADS),
        in_specs=[pl.BlockSpec(memory_space=pltpu.SMEM)],
        out_specs=pl.BlockSpec((1, 1, NA_PATTERNS, n_keys, NA_QBLK), lambda l, h: (l, h, 0, 0, 0)),
        out_shape=jax.ShapeDtypeStruct((DEPTH, NA_HEADS, NA_PATTERNS, n_keys, NA_QBLK), F32),
        compiler_params=_cparams(2),
        name="na_bias_tables",
    )(rpb_flat)


def _rope_tables(seq):
    half = DIFF_QK_DIM // 2
    t = np.arange(seq)
    pos = np.stack([t // GRID_W, t % GRID_W], axis=1).astype(np.float64)
    lane = np.arange(LANES)
    live = lane < DIFF_V_DIM
    seg = np.where(live, lane // half, 0)
    i = lane % half
    inv = ROPE_BASE ** (-(2.0 * (i % (half // 2))) / half)
    ang = pos[:, seg % 2] * inv[None, :]
    cos = np.where(live[None, :], np.cos(ang), 1.0)
    sgn = np.where(i < half // 2, -1.0, 1.0)
    sin = np.where(live[None, :], np.sin(ang) * sgn[None, :], 0.0)
    return jnp.asarray(cos, F32), jnp.asarray(sin, F32)


def _inproj_kernel(*refs, rope):
    if rope:
        x_ref, ss_ref, gpre_ref, w_ref, cos_ref, sin_ref = refs[:6]
        outs = refs[6:]
    else:
        x_ref, ss_ref, gpre_ref, w_ref = refs[:4]
        outs = refs[4:]
    kn_ref, kd_ref, qn_ref, qd_ref, vnt_ref, vdt_ref, gate_ref, u_ref = outs

    x = x_ref[0]
    ms = jnp.mean(x * x, axis=-1, keepdims=True)
    hn = x * lax.rsqrt(ms + EPS) * gpre_ref[...]
    ss = ss_ref[0, 0]
    hb = (hn * (1.0 + ss[:, D_MODEL:]) + ss[:, :D_MODEL]).astype(BF16)

    def proj(a, b):
        return _dot(hb, w_ref[0, :, a:b])

    if rope:
        cos = cos_ref[...]
        sin = sin_ref[...]
        lane = lax.broadcasted_iota(jnp.int32, cos.shape, 1)
        quarter = DIFF_QK_DIM // 4
        first = (lane % (2 * quarter)) < quarter

        def rot(zb):
            up = pltpu.roll(zb, LANES - quarter, axis=1)
            dn = pltpu.roll(zb, quarter, axis=1)
            return zb * cos + jnp.where(first, up, dn) * sin
    else:
        def rot(zb):
            return zb

    def store_kn(z):
        for p in range(NA_PAIRS):
            kn_ref[0, p] = z[:, p * LANES:(p + 1) * LANES].astype(BF16)

    def store_kd(z):
        for h in range(DIFF_HEADS):
            kd_ref[0, h] = rot(z[:, h * LANES:(h + 1) * LANES]).astype(BF16)

    def store_qn(z):
        for p in range(NA_PAIRS):
            qn_ref[0, p] = (z[:, p * LANES:(p + 1) * LANES] * NA_Q_SCALE).astype(BF16)

    def store_qd(z):
        for h in range(DIFF_HEADS):
            qd_ref[0, h] = (rot(z[:, h * LANES:(h + 1) * LANES]) * DIFF_Q_SCALE).astype(BF16)

    def store_vn(z):
        zt = z.T
        for p in range(NA_PAIRS):
            vnt_ref[0, p] = zt[p * LANES:(p + 1) * LANES].astype(BF16)

    def store_vd(z):
        zt = z.T
        for h in range(DIFF_HEADS):
            vdt_ref[0, h] = zt[h * DIFF_V_DIM:(h + 1) * DIFF_V_DIM].astype(BF16)

    def store_gate(z):
        gate_ref[0] = _silu(z)

    def store_u(ab):
        u_ref[0] = ab[0] * jax.nn.sigmoid(ab[1])

    z = proj(C_KN, C_VN)
    store_kn(z[:, C_KN:C_KD])
    store_kd(z[:, C_KD:C_QN])
    store_qn(z[:, C_QN:C_QD])
    store_qd(z[:, C_QD:C_VN])
    z = proj(C_VN, C_GATE)
    store_vn(z[:, :NA_WIDTH])
    store_vd(z[:, NA_WIDTH:])
    store_gate(proj(C_GATE, C_CA))
    z = proj(C_CA, IN_COLS_PADDED)
    store_u((z[:, :CONV_WIDTH], z[:, CONV_WIDTH:]))


def _inproj_io(b, s, ss, g_pre, w, layer, tables, tm):
    rope = tables is not None
    ss_idx = (lambda bi, i: (bi, 0, 0, 0)) if ss.shape[0] == b else (lambda bi, i: (0, 0, 0, 0))
    in_specs = [
        pl.BlockSpec((1, 1, 1, 2 * D_MODEL), ss_idx),
        pl.BlockSpec((1, D_MODEL), lambda bi, i: (0, 0)),
        pl.BlockSpec((1, D_MODEL, IN_COLS_PADDED), lambda bi, i: (layer, 0, 0)),
    ]
    args = [ss, g_pre.reshape(1, D_MODEL), w]
    if rope:
        in_specs += [pl.BlockSpec((tm, LANES), lambda bi, i: (i, 0))] * 2
        args += list(tables)
    row_major = lambda n: pl.BlockSpec((1, n, tm, LANES), lambda bi, i: (bi, 0, i, 0))
    out_specs = [
        row_major(NA_PAIRS), row_major(DIFF_HEADS), row_major(NA_PAIRS), row_major(DIFF_HEADS),
        pl.BlockSpec((1, NA_PAIRS, LANES, tm), lambda bi, i: (bi, 0, 0, i)),
        pl.BlockSpec((1, DIFF_HEADS, DIFF_V_DIM, tm), lambda bi, i: (bi, 0, 0, i)),
        pl.BlockSpec((1, tm, D_MODEL), lambda bi, i: (bi, i, 0)),
        pl.BlockSpec((1, tm, CONV_WIDTH), lambda bi, i: (bi, i, 0)),
    ]
    out_shape = [
        jax.ShapeDtypeStruct((b, NA_PAIRS, s, LANES), BF16),
        jax.ShapeDtypeStruct((b, DIFF_HEADS, s, LANES), BF16),
        jax.ShapeDtypeStruct((b, NA_PAIRS, s, LANES), BF16),
        jax.ShapeDtypeStruct((b, DIFF_HEADS, s, LANES), BF16),
        jax.ShapeDtypeStruct((b, NA_PAIRS, LANES, s), BF16),
        jax.ShapeDtypeStruct((b, DIFF_HEADS, DIFF_V_DIM, s), BF16),
        jax.ShapeDtypeStruct((b, s, D_MODEL), F32),
        jax.ShapeDtypeStruct((b, s, CONV_WIDTH), F32),
    ]
    return in_specs, args, out_specs, out_shape


def _inproj(x, ss, g_pre, w, layer, tables, tm):
    b, s, _ = x.shape
    in_specs, args, out_specs, out_shape = _inproj_io(b, s, ss, g_pre, w, layer, tables, tm)
    return pl.pallas_call(
        functools.partial(_inproj_kernel, rope=tables is not None),
        grid=(b, s // tm),
        in_specs=[pl.BlockSpec((1, tm, D_MODEL), lambda bi, i: (bi, i, 0))] + in_specs,
        out_specs=out_specs,
        out_shape=out_shape,
        compiler_params=_cparams(2),
        name="inproj_lat" if tables is not None else "inproj_ctx",
    )(x, *args)


def _na_heads(groups, s_ref):
    lane = lax.broadcasted_iota(jnp.int32, (NA_QBLK, LANES), 1)
    tasks = [(g, head) for g in range(len(groups)) for head in range(NA_HEADS)]

    def masked_queries(t):
        g, head = tasks[t]
        q = groups[g][0](head // 2)
        lo = (head % 2) * NA_HEAD_DIM
        return jnp.where((lane >= lo) & (lane < lo + NA_HEAD_DIM), q, jnp.zeros_like(q))

    def score_block(t, c, qm, m):
        g, head = tasks[t]
        s = _dot_nt(groups[g][1][c][0, head // 2], qm)
        if c > 0:
            s = s + groups[g][3](head, c - 1)
        s_ref[t % 2, c * KEY_BLK:(c + 1) * KEY_BLK, :] = s
        mc = jnp.max(s, axis=0, keepdims=True)
        m[0] = mc if m[0] is None else jnp.maximum(m[0], mc)

    def exp_block(t, c, m, denom):
        e = jnp.exp2(s_ref[t % 2, c * KEY_BLK:(c + 1) * KEY_BLK, :] - m[0])
        dc = jnp.sum(e, axis=0, keepdims=True)
        denom[0] = dc if denom[0] is None else denom[0] + dc
        return e.astype(BF16)

    def head_rows(head):
        return slice((head % 2) * NA_HEAD_DIM, (head % 2 + 1) * NA_HEAD_DIM)

    def pv_block(t, c, e, acc):
        g, head = tasks[t]
        oc = _dot(groups[g][2][c][0, head // 2, head_rows(head), :], e)
        acc[0] = oc if acc[0] is None else acc[0] + oc

    n_blk = len(groups[0][1])
    m_next = [None]
    qm = masked_queries(0)
    for c in range(n_blk):
        score_block(0, c, qm, m_next)
    for t, (g, head) in enumerate(tasks):
        m, m_next = m_next, [None]
        denom, acc = [None], [None]
        if t + 1 < len(tasks):
            qm = masked_queries(t + 1)
        waiting = []
        for c in range(n_blk):
            if t + 1 < len(tasks):
                score_block(t + 1, c, qm, m_next)
            waiting.append((c, exp_block(t, c, m, denom)))
            if len(waiting) > PV_LAG:
                pv_block(t, *waiting.pop(0), acc)
        for item in waiting:
            pv_block(t, *item, acc)
        groups[g][4](head // 2, head_rows(head), acc[0] * (1.0 / denom[0]))


NA_QBLKS_PER_STEP = 2


def _na_lat_kernel(*refs):
    n = NA_QBLKS_PER_STEP
    q_ref, kc_ref = refs[0], refs[1]
    k_refs = refs[2:2 + n * NA_LOC_BLKS]
    vc_ref = refs[2 + n * NA_LOC_BLKS]
    v_refs = refs[3 + n * NA_LOC_BLKS:3 + 2 * n * NA_LOC_BLKS]
    bias_refs = refs[3 + 2 * n * NA_LOC_BLKS:3 + 2 * n * NA_LOC_BLKS + n]
    o_ref, s_ref = refs[-2], refs[-1]

    def group(a):
        cols = slice(a * NA_QBLK, (a + 1) * NA_QBLK)

        def store(pair, rows, value):
            o_ref[0, pair, rows, cols] = value

        return (lambda pair: q_ref[0, pair, cols, :],
                (kc_ref,) + tuple(k_refs[a * NA_LOC_BLKS:(a + 1) * NA_LOC_BLKS]),
                (vc_ref,) + tuple(v_refs[a * NA_LOC_BLKS:(a + 1) * NA_LOC_BLKS]),
                lambda head, c: bias_refs[a][0, head, 0, c * KEY_BLK:(c + 1) * KEY_BLK, :],
                store)

    _na_heads([group(a) for a in range(n)], s_ref)


def _na_ctx_kernel(q_ref, kc_ref, vc_ref, o_ref, s_ref):
    def store(pair, rows, value):
        o_ref[0, pair, rows, :] = value

    _na_heads([(lambda pair: q_ref[0, pair], (kc_ref,), (vc_ref,), None, store)], s_ref)


def _na_lat(qn, kn_c, kn, vnt_c, vnt, bias, layer):
    b, _, t, _ = qn.shape
    n = NA_QBLKS_PER_STEP
    n_qblk = t // NA_QBLK
    last_start = t // KEY_BLK - NA_LOC_BLKS

    def start(j):
        return jnp.clip(j - 1, 0, last_start)

    def pattern(j):
        return jnp.where(j == 0, 0, jnp.where(j == n_qblk - 1, 2, 1))

    row_blk = (1, NA_PAIRS, KEY_BLK, LANES)
    col_blk = (1, NA_PAIRS, LANES, KEY_BLK)
    k_specs = [pl.BlockSpec(row_blk, functools.partial(lambda bi, j, a, i: (bi, 0, start(n * j + a) + i, 0), a=a, i=i))
               for a in range(n) for i in range(NA_LOC_BLKS)]
    v_specs = [pl.BlockSpec(col_blk, functools.partial(lambda bi, j, a, i: (bi, 0, 0, start(n * j + a) + i), a=a, i=i))
               for a in range(n) for i in range(NA_LOC_BLKS)]
    bias_specs = [pl.BlockSpec((1, NA_HEADS, 1, NA_LOC_BLKS * KEY_BLK, NA_QBLK),
                               functools.partial(lambda bi, j, a: (layer, 0, pattern(n * j + a), 0, 0), a=a))
                  for a in range(n)]
    return pl.pallas_call(
        _na_lat_kernel,
        grid=(b, n_qblk // n),
        in_specs=[pl.BlockSpec((1, NA_PAIRS, n * NA_QBLK, LANES), lambda bi, j: (bi, 0, j, 0)),
                  pl.BlockSpec(row_blk, lambda bi, j: (bi, 0, 0, 0))] + k_specs
                 + [pl.BlockSpec(col_blk, lambda bi, j: (bi, 0, 0, 0))] + v_specs + bias_specs,
        out_specs=pl.BlockSpec((1, NA_PAIRS, LANES, n * NA_QBLK), lambda bi, j: (bi, 0, 0, j)),
        out_shape=jax.ShapeDtypeStruct((b, NA_PAIRS, LANES, t), F32),
        scratch_shapes=[pltpu.VMEM((2, (1 + NA_LOC_BLKS) * KEY_BLK, NA_QBLK), F32)],
        compiler_params=_cparams(2),
        name="na_lat",
    )(qn, kn_c, *([kn] * (n * NA_LOC_BLKS)), vnt_c, *([vnt] * (n * NA_LOC_BLKS)), *([bias] * n))


def _na_ctx(qn_c, kn_c, vnt_c):
    b, _, c, _ = qn_c.shape
    row_blk = (1, NA_PAIRS, c, LANES)
    col_blk = (1, NA_PAIRS, LANES, c)
    return pl.pallas_call(
        _na_ctx_kernel,
        grid=(b,),
        in_specs=[pl.BlockSpec(row_blk, lambda bi: (bi, 0, 0, 0)),
                  pl.BlockSpec(row_blk, lambda bi: (bi, 0, 0, 0)),
                  pl.BlockSpec(col_blk, lambda bi: (bi, 0, 0, 0))],
        out_specs=pl.BlockSpec(col_blk, lambda bi: (bi, 0, 0, 0)),
        out_shape=jax.ShapeDtypeStruct((b, NA_PAIRS, LANES, c), F32),
        scratch_shapes=[pltpu.VMEM((2, c, c), F32)],
        compiler_params=_cparams(1),
        name="na_ctx",
    )(qn_c, kn_c, vnt_c)


DIFF_SUB = 256
DIFF_KEY_BLK = 256


def _diff_body(q_ref, block_rows, key_blocks, val_blocks, lam_ref, g_ref, o_ref, s_ref, lam_init):
    n_sub = q_ref.shape[2] // DIFF_SUB
    edges = np.cumsum([0] + list(block_rows))
    rows = [slice(int(a), int(b)) for a, b in zip(edges[:-1], edges[1:])]
    lp = lam_ref[...]
    lam = (jnp.exp(jnp.sum(lp[0:1] * lp[1:2], axis=1, keepdims=True))
           - jnp.exp(jnp.sum(lp[2:3] * lp[3:4], axis=1, keepdims=True)) + lam_init)
    gain = g_ref[...] * (1.0 - lam_init)

    def masked_queries(i):
        q = q_ref[0, 0, i * DIFF_SUB:(i + 1) * DIFF_SUB, :]
        lane = lax.broadcasted_iota(jnp.int32, q.shape, 1)
        zero = jnp.zeros_like(q)
        return (jnp.where(lane < DIFF_QK_DIM, q, zero),
                jnp.where((lane >= DIFF_QK_DIM) & (lane < 2 * DIFF_QK_DIM), q, zero))

    def score_block(i, c, qm, m):
        k = key_blocks[c]()
        for mp in range(2):
            s = _dot_nt(k, qm[mp])
            s_ref[i % 2, mp, rows[c], :] = s
            mc = jnp.max(s, axis=0, keepdims=True)
            m[mp] = mc if m[mp] is None else jnp.maximum(m[mp], mc)

    def exp_block(i, c, m, l):
        ps = []
        for mp in range(2):
            p = jnp.exp2(s_ref[i % 2, mp, rows[c], :] - m[mp])
            lc = jnp.sum(p, axis=0, keepdims=True)
            l[mp] = lc if l[mp] is None else l[mp] + lc
            ps.append(p.astype(BF16))
        return ps

    def pv_block(c, ps, acc):
        vt = val_blocks[c]()
        for mp in range(2):
            pv = _dot(vt, ps[mp])
            acc[mp] = pv if acc[mp] is None else acc[mp] + pv

    n_blk = len(key_blocks)
    m_next = [None, None]
    qm = masked_queries(0)
    for c in range(n_blk):
        score_block(0, c, qm, m_next)
    for i in range(n_sub):
        m, m_next = m_next, [None, None]
        l, acc = [None, None], [None, None]
        if i + 1 < n_sub:
            qm = masked_queries(i + 1)
        waiting = []
        for c in range(n_blk):
            if i + 1 < n_sub:
                score_block(i + 1, c, qm, m_next)
            waiting.append((c, exp_block(i, c, m, l)))
            if len(waiting) > PV_LAG:
                pv_block(*waiting.pop(0), acc)
        for item in waiting:
            pv_block(*item, acc)
        out = acc[0] * (1.0 / l[0]) - acc[1] * (lam / l[1])
        ms = jnp.mean(out * out, axis=0, keepdims=True)
        o_ref[0, 0, :, i * DIFF_SUB:(i + 1) * DIFF_SUB] = out * lax.rsqrt(ms + EPS) * gain


def _diff_lat_kernel(q_ref, kc_ref, k_ref, vc_ref, v_ref, lam_ref, g_ref, o_ref, s_ref, *, lam_init):
    n_lat = k_ref.shape[2] // DIFF_KEY_BLK
    blk = DIFF_KEY_BLK
    keys = [lambda: kc_ref[0, 0]] + [
        functools.partial(lambda c: k_ref[0, 0, c * blk:(c + 1) * blk, :], c) for c in range(n_lat)]
    vals = [lambda: vc_ref[0, 0]] + [
        functools.partial(lambda c: v_ref[0, 0, :, c * blk:(c + 1) * blk], c) for c in range(n_lat)]
    _diff_body(q_ref, [kc_ref.shape[2]] + [blk] * n_lat, keys, vals, lam_ref, g_ref, o_ref, s_ref, lam_init)


def _diff_ctx_kernel(q_ref, kc_ref, vc_ref, lam_ref, g_ref, o_ref, s_ref, *, lam_init):
    _diff_body(q_ref, [kc_ref.shape[2]], [lambda: kc_ref[0, 0]], [lambda: vc_ref[0, 0]], lam_ref, g_ref, o_ref,
               s_ref, lam_init)


def _diff_scratch(n_keys):
    return [pltpu.VMEM((2, 2, n_keys, DIFF_SUB), F32)]


def _diff_lat(qd, kd_c, kd, vdt_c, vdt, lam_p, g_col, lam_init, qb):
    b, _, t, _ = qd.shape
    c = kd_c.shape[2]
    one_head = lambda shape, idx: pl.BlockSpec((1, 1) + shape, idx)
    return pl.pallas_call(
        functools.partial(_diff_lat_kernel, lam_init=lam_init),
        grid=(b, DIFF_HEADS, t // qb),
        in_specs=[one_head((qb, LANES), lambda bi, h, j: (bi, h, j, 0)),
                  one_head((c, LANES), lambda bi, h, j: (bi, h, 0, 0)),
                  one_head((t, LANES), lambda bi, h, j: (bi, h, 0, 0)),
                  one_head((DIFF_V_DIM, c), lambda bi, h, j: (bi, h, 0, 0)),
                  one_head((DIFF_V_DIM, t), lambda bi, h, j: (bi, h, 0, 0)),
                  pl.BlockSpec(lam_p.shape, lambda bi, h, j: (0, 0)),
                  pl.BlockSpec(g_col.shape, lambda bi, h, j: (0, 0))],
        out_specs=one_head((DIFF_V_DIM, qb), lambda bi, h, j: (bi, h, 0, j)),
        out_shape=jax.ShapeDtypeStruct((b, DIFF_HEADS, DIFF_V_DIM, t), F32),
        scratch_shapes=_diff_scratch(c + t),
        compiler_params=_cparams(3),
        name="diff_lat",
    )(qd, kd_c, kd, vdt_c, vdt, lam_p, g_col)


def _diff_ctx(qd_c, kd_c, vdt_c, lam_p, g_col, lam_init):
    b, _, c, _ = qd_c.shape
    one_head = lambda shape: pl.BlockSpec((1, 1) + shape, lambda bi, h: (bi, h, 0, 0))
    return pl.pallas_call(
        functools.partial(_diff_ctx_kernel, lam_init=lam_init),
        grid=(b, DIFF_HEADS),
        in_specs=[one_head((c, LANES)), one_head((c, LANES)), one_head((DIFF_V_DIM, c)),
                  pl.BlockSpec(lam_p.shape, lambda bi, h: (0, 0)),
                  pl.BlockSpec(g_col.shape, lambda bi, h: (0, 0))],
        out_specs=one_head((DIFF_V_DIM, c)),
        out_shape=jax.ShapeDtypeStruct((b, DIFF_HEADS, DIFF_V_DIM, c), F32),
        scratch_shapes=_diff_scratch(c),
        compiler_params=_cparams(2),
        name="diff_ctx",
    )(qd_c, kd_c, vdt_c, lam_p, g_col)


CONV_PAD = 16
CONV_TILE = 256
SUBLANES = 8


def _conv_kernel(u_ref, w_ref, b_ref, g_ref, beta_ref, o_ref, sh_ref):
    s = u_ref.shape[1]
    zeros = jnp.zeros((CONV_PAD, CONV_WIDTH), F32)
    sh_ref[0, 0:CONV_PAD, :] = zeros
    sh_ref[0, CONV_PAD + s:CONV_PAD + s + CONV_PAD, :] = zeros
    sh_ref[0, CONV_PAD:CONV_PAD + s, :] = u_ref[0]
    first = CONV_PAD - CONV_TAPS // 2
    n_rows = s + 2 * CONV_PAD - SUBLANES
    for r in range(1, SUBLANES):
        sh_ref[r, 0:n_rows, :] = sh_ref[0, r:r + n_rows, :]
    for t0 in range(0, s, CONV_TILE):
        acc = jnp.broadcast_to(b_ref[...], (CONV_TILE, CONV_WIDTH))
        for j in range(CONV_TAPS):
            r = (first + j) % SUBLANES
            base = t0 + first + j - r
            acc = acc + sh_ref[r, base:base + CONV_TILE, :] * w_ref[j:j + 1, :]
        mu = jnp.mean(acc, axis=-1, keepdims=True)
        d = acc - mu
        var = jnp.mean(d * d, axis=-1, keepdims=True)
        y = d * lax.rsqrt(var + EPS) * g_ref[...] + beta_ref[...]
        o_ref[0, t0:t0 + CONV_TILE, :] = _silu(y)


def _conv(u, conv_w, conv_b, ln_g, ln_b):
    b, s, _ = u.shape
    row = lambda a: a.reshape(1, CONV_WIDTH)
    vec_spec = pl.BlockSpec((1, CONV_WIDTH), lambda bi: (0, 0))
    return pl.pallas_call(
        _conv_kernel,
        grid=(b,),
        in_specs=[pl.BlockSpec((1, s, CONV_WIDTH), lambda bi: (bi, 0, 0)),
                  pl.BlockSpec((CONV_TAPS, CONV_WIDTH), lambda bi: (0, 0)),
                  vec_spec, vec_spec, vec_spec],
        out_specs=pl.BlockSpec((1, s, CONV_WIDTH), lambda bi: (bi, 0, 0)),
        out_shape=jax.ShapeDtypeStruct((b, s, CONV_WIDTH), F32),
        scratch_shapes=[pltpu.VMEM((SUBLANES, s + 2 * CONV_PAD, CONV_WIDTH), F32)],
        compiler_params=_cparams(1),
        name="conformer_conv",
    )(u, conv_w, row(conv_b), row(ln_g), row(ln_b))


def _outproj_kernel(ont_ref, odt_ref, cv_ref, gate_ref, x_ref, w_ref, gpost_ref, gmod_ref, o_ref):
    tm = x_ref.shape[1]
    a = ont_ref[0].reshape(NA_WIDTH, tm).T
    bq = odt_ref[0].reshape(DIFF_WIDTH, tm).T
    mixed = (jnp.concatenate([a, bq, cv_ref[0]], axis=1) * gate_ref[0]).astype(BF16)
    y = _dot(mixed, w_ref[0])
    ms = jnp.mean(y * y, axis=-1, keepdims=True)
    o_ref[0] = x_ref[0] + gmod_ref[0] * (y * lax.rsqrt(ms + EPS) * gpost_ref[...])


def _outproj_io(ont, odt, cv, gate, x, w, layer, g_post, gmod, tm):
    b, s, _ = x.shape
    gm_idx = (lambda bi, i: (bi, 0, 0)) if gmod.shape[0] == b else (lambda bi, i: (0, 0, 0))
    rows = lambda width: pl.BlockSpec((1, tm, width), lambda bi, i: (bi, i, 0))
    in_specs = [pl.BlockSpec((1, NA_PAIRS, LANES, tm), lambda bi, i: (bi, 0, 0, i)),
                pl.BlockSpec((1, DIFF_HEADS, DIFF_V_DIM, tm), lambda bi, i: (bi, 0, 0, i)),
                rows(CONV_WIDTH), rows(D_MODEL), rows(D_MODEL),
                pl.BlockSpec((1, D_MODEL, D_MODEL), lambda bi, i: (layer, 0, 0)),
                pl.BlockSpec((1, D_MODEL), lambda bi, i: (0, 0)),
                pl.BlockSpec((1, 1, D_MODEL), gm_idx)]
    args = [ont, odt, cv, gate, x, w, g_post.reshape(1, D_MODEL), gmod]
    return in_specs, args, rows(D_MODEL), jax.ShapeDtypeStruct((b, s, D_MODEL), F32)


def _outproj(ont, odt, cv, gate, x, w, layer, g_post, gmod, tm):
    in_specs, args, out_spec, out_shape = _outproj_io(ont, odt, cv, gate, x, w, layer, g_post, gmod, tm)
    return pl.pallas_call(
        _outproj_kernel,
        grid=(x.shape[0], x.shape[1] // tm),
        in_specs=in_specs,
        out_specs=out_spec,
        out_shape=out_shape,
        compiler_params=_cparams(2),
        name="outproj",
    )(*args)


def _outin_kernel(*refs, rope):
    n_out_in = 8
    n_in_in = 5 if rope else 3
    out_in = refs[:n_out_in]
    in_in = refs[n_out_in:n_out_in + n_in_in]
    x_new_ref = refs[n_out_in + n_in_in]
    in_out = refs[n_out_in + n_in_in + 1:]
    _outproj_kernel(*out_in, x_new_ref)
    _inproj_kernel(x_new_ref, *in_in, *in_out, rope=rope)


def _outproj_inproj(out_args, in_args, tm):
    x = out_args[4]
    b, s, _ = x.shape
    o_in_specs, o_args, o_out_spec, o_out_shape = _outproj_io(*out_args, tm)
    i_in_specs, i_args, i_out_specs, i_out_shape = _inproj_io(b, s, *in_args, tm)
    rope = in_args[-1] is not None
    return pl.pallas_call(
        functools.partial(_outin_kernel, rope=rope),
        grid=(b, s // tm),
        in_specs=o_in_specs + i_in_specs,
        out_specs=[o_out_spec] + i_out_specs,
        out_shape=[o_out_shape] + i_out_shape,
        compiler_params=_cparams(2),
        name="outin_lat" if rope else "outin_ctx",
    )(*o_args, *i_args)


def _relayout_w_in(w_in):
    edges = np.cumsum((0, NA_WIDTH, NA_WIDTH, DIFF_WIDTH, DIFF_WIDTH, NA_WIDTH, NA_WIDTH, DIFF_WIDTH,
                       DIFF_WIDTH, CONV_WIDTH, CONV_WIDTH, CONV_WIDTH))
    na_k, na_v, df_k, df_v, na_q, na_g, df_q, df_g, cv_a, cv_b, cv_g = [
        w_in[..., a:b] for a, b in zip(edges[:-1], edges[1:])]

    def pad_heads(w):
        w = w.reshape(w.shape[:-1] + (DIFF_HEADS, DIFF_V_DIM))
        w = jnp.pad(w, ((0, 0),) * (w.ndim - 1) + ((0, LANES - DIFF_V_DIM),))
        return w.reshape(w.shape[:-2] + (DIFF_HEADS * LANES,))

    return jnp.concatenate([na_k, pad_heads(df_k), na_q, pad_heads(df_q), na_v, df_v,
                            na_g, df_g, cv_g, cv_a, cv_b], axis=-1).astype(BF16)


PROJ_TM = 512
DIFF_QB = 2048


def kernel(x, c, ctx, c_ctx, w_mod, b_mod, g_pre, g_post, w_in, w_out, na_rpb, diff_lq1, diff_lk1,
           diff_lq2, diff_lk2, diff_subln_g, conv_w, conv_b, conv_ln_g, conv_ln_b):
    batch, seq, _ = x.shape
    pad_rows = (-(batch + 1)) % 8
    cc = jnp.concatenate([c, c_ctx[None, :], jnp.zeros((pad_rows, D_MODEL), F32)], axis=0)
    mod = _modulation(cc, w_mod, b_mod)
    bias = _na_bias_tables(na_rpb.reshape(-1))
    w_in_b = _relayout_w_in(w_in)
    w_out_b = w_out.astype(BF16)
    tables = _rope_tables(seq)
    lam_p = jnp.stack([diff_lq1, diff_lk1, diff_lq2, diff_lk2], axis=1)

    xc = ctx
    ctx_tm = xc.shape[1]

    def in_args(l, latent):
        ss = mod[l, :batch, :2 * D_MODEL].reshape(batch, 1, 1, 2 * D_MODEL) if latent else \
            mod[l, batch, :2 * D_MODEL].reshape(1, 1, 1, 2 * D_MODEL)
        return ss, g_pre[l], w_in_b, l, tables if latent else None

    lat = _inproj(x, *in_args(0, True), PROJ_TM)
    cx = _inproj(xc, *in_args(0, False), ctx_tm)
    for l in range(DEPTH):
        last = l == DEPTH - 1
        lam_init = 0.8 - 0.6 * math.exp(-0.3 * l)
        g_col = diff_subln_g[l].reshape(DIFF_V_DIM, 1)
        kn, kd, qn, qd, vnt, vdt, gate, u = lat
        kn_c, kd_c, qn_c, qd_c, vnt_c, vdt_c, gate_c, u_c = cx

        ont = _na_lat(qn, kn_c, kn, vnt_c, vnt, bias, l)
        odt = _diff_lat(qd, kd_c, kd, vdt_c, vdt, lam_p[l], g_col, lam_init, DIFF_QB)
        cv = _conv(u, conv_w[l], conv_b[l], conv_ln_g[l], conv_ln_b[l])
        gmod = mod[l, :batch, 2 * D_MODEL:].reshape(batch, 1, D_MODEL)
        out_args = (ont, odt, cv, gate, x, w_out_b, l, g_post[l], gmod)
        if last:
            return _outproj(*out_args, PROJ_TM)

        ont_c = _na_ctx(qn_c, kn_c, vnt_c)
        odt_c = _diff_ctx(qd_c, kd_c, vdt_c, lam_p[l], g_col, lam_init)
        cv_c = _conv(u_c, conv_w[l], conv_b[l], conv_ln_g[l], conv_ln_b[l])
        gmod_c = mod[l, batch, 2 * D_MODEL:].reshape(1, 1, D_MODEL)
        out_args_c = (ont_c, odt_c, cv_c, gate_c, xc, w_out_b, l, g_post[l], gmod_c)

        x, *lat = _outproj_inproj(out_args, in_args(l + 1, True), PROJ_TM)
        xc, *cx = _outproj_inproj(out_args_c, in_args(l + 1, False), ctx_tm)
```

```python
import functools
import math

import numpy as np
import jax
import jax.numpy as jnp
from jax import lax
from jax.experimental import pallas as pl
from jax.experimental.pallas import tpu as pltpu

F32 = jnp.float32
BF16 = jnp.bfloat16

D_MODEL = 1024
DEPTH = 4
GRID_W = 64
NA_HEADS = 6
NA_HEAD_DIM = 64
NA_WIDTH = NA_HEADS * NA_HEAD_DIM
NA_WIN_H = 8
NA_WIN_W = 16
DIFF_HEADS = 4
DIFF_QK_DIM = 48
DIFF_V_DIM = 2 * DIFF_QK_DIM
DIFF_WIDTH = DIFF_HEADS * DIFF_V_DIM
CONV_WIDTH = D_MODEL - NA_WIDTH - DIFF_WIDTH
CONV_TAPS = 31
ROPE_BASE = 10000.0
EPS = 1e-6
NEG_INF = -1e30
LOG2_E = math.log2(math.e)
DIFF_Q_SCALE = DIFF_QK_DIM ** -0.5 * LOG2_E
NA_Q_SCALE = NA_HEAD_DIM ** -0.5 * LOG2_E

LANES = 128
NA_PAIRS = NA_HEADS // 2
KEY_BLK = 256
NA_QBLK = 256
NA_ROWS_PER_QBLK = NA_QBLK // GRID_W
NA_LOC_BLKS = 3
NA_PATTERNS = 3
VMEM_LIMIT = 56 * 1024 * 1024

C_KN = 0
C_KD = C_KN + NA_WIDTH
C_QN = C_KD + DIFF_HEADS * LANES
C_QD = C_QN + NA_WIDTH
C_VN = C_QD + DIFF_HEADS * LANES
C_VD = C_VN + NA_WIDTH
C_GATE = C_VD + DIFF_WIDTH
C_CA = C_GATE + D_MODEL
C_CB = C_CA + CONV_WIDTH
IN_COLS_PADDED = C_CB + CONV_WIDTH


def _cparams(n_axes, vmem=VMEM_LIMIT):
    return pltpu.CompilerParams(dimension_semantics=("arbitrary",) * n_axes, vmem_limit_bytes=vmem)


def _silu(z):
    return z * jax.nn.sigmoid(z)


def _dot(a, b):
    return jnp.dot(a, b, preferred_element_type=F32)


def _dot_nt(a, b):
    return lax.dot_general(a, b, (((1,), (1,)), ((), ())), preferred_element_type=F32)


PV_LAG = 2


def _mod_kernel(cc_ref, w_ref, b_ref, o_ref):
    s = _silu(cc_ref[...])
    o_ref[0] = _dot(s, w_ref[0]) + b_ref[0]


def _modulation(cc, w_mod, b_mod):
    n_rows = cc.shape[0]
    return pl.pallas_call(
        _mod_kernel,
        grid=(DEPTH, 3),
        in_specs=[
            pl.BlockSpec((n_rows, D_MODEL), lambda l, j: (0, 0)),
            pl.BlockSpec((1, D_MODEL, D_MODEL), lambda l, j: (l, 0, j)),
            pl.BlockSpec((1, 1, D_MODEL), lambda l, j: (l, 0, j)),
        ],
        out_specs=pl.BlockSpec((1, n_rows, D_MODEL), lambda l, j: (l, 0, j)),
        out_shape=jax.ShapeDtypeStruct((DEPTH, n_rows, 3 * D_MODEL), F32),
        compiler_params=_cparams(2),
        name="adaln_mod",
    )(cc, w_mod, b_mod.reshape(DEPTH, 1, 3 * D_MODEL))


def _na_bias_kernel(rpb_ref, o_ref):
    l = pl.program_id(0)
    h = pl.program_id(1)
    n_dr = 2 * NA_WIN_H - 1
    n_dc = 2 * NA_WIN_W - 1
    base = (l * NA_HEADS + h) * (n_dr * n_dc)
    kc = lax.broadcasted_iota(jnp.int32, (GRID_W, LANES), 0)
    lane = lax.broadcasted_iota(jnp.int32, (GRID_W, LANES), 1)
    qc = lane % GRID_W
    cstart = jnp.clip(qc - NA_WIN_W // 2, 0, GRID_W - NA_WIN_W)
    col_ok = (kc >= cstart) & (kc < cstart + NA_WIN_W)
    dc = kc - qc + (NA_WIN_W - 1)
    neg = jnp.full((GRID_W, LANES), NEG_INF, F32)
    dc_masks = [col_ok & (dc == i) for i in range(n_dc)]
    tiles = []
    for dr in range(n_dr):
        t = neg
        for i in range(n_dc):
            t = jnp.where(dc_masks[i], rpb_ref[base + dr * n_dc + i], t)
        tiles.append(t * LOG2_E)
    left = lane < GRID_W

    def tile(dr, ok):
        return tiles[dr] if ok else neg

    patterns = (
        (lambda i: (0, NA_WIN_H), NA_WIN_H - 1),
        (lambda i: (i, i + NA_WIN_H), NA_WIN_H - 1 - NA_ROWS_PER_QBLK),
        (lambda i: (4, 4 + NA_WIN_H), -1),
    )
    for p, (rng, off) in enumerate(patterns):
        for j in range(NA_LOC_BLKS * KEY_BLK // GRID_W):
            halves = []
            for i0 in (0, 2):
                pair = []
                for i in (i0, i0 + 1):
                    lo, hi = rng(i)
                    pair.append(tile(j - i + off, lo <= j < hi) if 0 <= j - i + off < n_dr else neg)
                halves.append(jnp.where(left, pair[0], pair[1]))
            o_ref[0, 0, p, j * GRID_W:(j + 1) * GRID_W, :] = jnp.concatenate(halves, axis=1)


def _na_bias_tables(rpb_flat):
    n_keys = NA_LOC_BLKS * KEY_BLK
    return pl.pallas_call(
        _na_bias_kernel,
        grid=(DEPTH, NA_HEADS),
        in_specs=[pl.BlockSpec(memory_space=pltpu.SMEM)],
        out_specs=pl.BlockSpec((1, 1, NA_PATTERNS, n_keys, NA_QBLK), lambda l, h: (l, h, 0, 0, 0)),
        out_shape=jax.ShapeDtypeStruct((DEPTH, NA_HEADS, NA_PATTERNS, n_keys, NA_QBLK), F32),
        compiler_params=_cparams(2),
        name="na_bias_tables",
    )(rpb_flat)


def _rope_tables(seq):
    half = DIFF_QK_DIM // 2
    t = np.arange(seq)
    pos = np.stack([t // GRID_W, t % GRID_W], axis=1).astype(np.float64)
    lane = np.arange(LANES)
    live = lane < DIFF_V_DIM
    seg = np.where(live, lane // half, 0)
    i = lane % half
    inv = ROPE_BASE ** (-(2.0 * (i % (half // 2))) / half)
    ang = pos[:, seg % 2] * inv[None, :]
    cos = np.where(live[None, :], np.cos(ang), 1.0)
    sgn = np.where(i < half // 2, -1.0, 1.0)
    sin = np.where(live[None, :], np.sin(ang) * sgn[None, :], 0.0)
    return jnp.asarray(cos, F32), jnp.asarray(sin, F32)


def _inproj_kernel(*refs, rope):
    if rope:
        x_ref, ss_ref, gpre_ref, w_ref, cos_ref, sin_ref = refs[:6]
        outs = refs[6:]
    else:
        x_ref, ss_ref, gpre_ref, w_ref = refs[:4]
        outs = refs[4:]
    kn_ref, kd_ref, qn_ref, qd_ref, vnt_ref, vdt_ref, gate_ref, u_ref = outs

    x = x_ref[0]
    ms = jnp.mean(x * x, axis=-1, keepdims=True)
    hn = x * lax.rsqrt(ms + EPS) * gpre_ref[...]
    ss = ss_ref[0, 0]
    hb = (hn * (1.0 + ss[:, D_MODEL:]) + ss[:, :D_MODEL]).astype(BF16)

    def proj(a, b):
        return _dot(hb, w_ref[0, :, a:b])

    if rope:
        cos = cos_ref[...]
        sin = sin_ref[...]
        lane = lax.broadcasted_iota(jnp.int32, cos.shape, 1)
        quarter = DIFF_QK_DIM // 4
        first = (lane % (2 * quarter)) < quarter

        def rot(zb):
            up = pltpu.roll(zb, LANES - quarter, axis=1)
            dn = pltpu.roll(zb, quarter, axis=1)
            return zb * cos + jnp.where(first, up, dn) * sin
    else:
        def rot(zb):
            return zb

    def store_kn(z):
        for p in range(NA_PAIRS):
            kn_ref[0, p] = z[:, p * LANES:(p + 1) * LANES].astype(BF16)

    def store_kd(z):
        for h in range(DIFF_HEADS):
            kd_ref[0, h] = rot(z[:, h * LANES:(h + 1) * LANES]).astype(BF16)

    def store_qn(z):
        for p in range(NA_PAIRS):
            qn_ref[0, p] = (z[:, p * LANES:(p + 1) * LANES] * NA_Q_SCALE).astype(BF16)

    def store_qd(z):
        for h in range(DIFF_HEADS):
            qd_ref[0, h] = (rot(z[:, h * LANES:(h + 1) * LANES]) * DIFF_Q_SCALE).astype(BF16)

    def store_vn(z):
        zt = z.T
        for p in range(NA_PAIRS):
            vnt_ref[0, p] = zt[p * LANES:(p + 1) * LANES].astype(BF16)

    def store_vd(z):
        zt = z.T
        for h in range(DIFF_HEADS):
            vdt_ref[0, h] = zt[h * DIFF_V_DIM:(h + 1) * DIFF_V_DIM].astype(BF16)

    def store_gate(z):
        gate_ref[0] = _silu(z)

    def store_u(ab):
        u_ref[0] = ab[0] * jax.nn.sigmoid(ab[1])

    z = proj(C_KN, C_VN)
    store_kn(z[:, C_KN:C_KD])
    store_kd(z[:, C_KD:C_QN])
    store_qn(z[:, C_QN:C_QD])
    store_qd(z[:, C_QD:C_VN])
    z = proj(C_VN, C_GATE)
    store_vn(z[:, :NA_WIDTH])
    store_vd(z[:, NA_WIDTH:])
    store_gate(proj(C_GATE, C_CA))
    z = proj(C_CA, IN_COLS_PADDED)
    store_u((z[:, :CONV_WIDTH], z[:, CONV_WIDTH:]))


def _inproj_io(b, s, ss, g_pre, w, layer, tables, tm):
    rope = tables is not None
    ss_idx = (lambda bi, i: (bi, 0, 0, 0)) if ss.shape[0] == b else (lambda bi, i: (0, 0, 0, 0))
    in_specs = [
        pl.BlockSpec((1, 1, 1, 2 * D_MODEL), ss_idx),
        pl.BlockSpec((1, D_MODEL), lambda bi, i: (0, 0)),
        pl.BlockSpec((1, D_MODEL, IN_COLS_PADDED), lambda bi, i: (layer, 0, 0)),
    ]
    args = [ss, g_pre.reshape(1, D_MODEL), w]
    if rope:
        in_specs += [pl.BlockSpec((tm, LANES), lambda bi, i: (i, 0))] * 2
        args += list(tables)
    row_major = lambda n: pl.BlockSpec((1, n, tm, LANES), lambda bi, i: (bi, 0, i, 0))
    out_specs = [
        row_major(NA_PAIRS), row_major(DIFF_HEADS), row_major(NA_PAIRS), row_major(DIFF_HEADS),
        pl.BlockSpec((1, NA_PAIRS, LANES, tm), lambda bi, i: (bi, 0, 0, i)),
        pl.BlockSpec((1, DIFF_HEADS, DIFF_V_DIM, tm), lambda bi, i: (bi, 0, 0, i)),
        pl.BlockSpec((1, tm, D_MODEL), lambda bi, i: (bi, i, 0)),
        pl.BlockSpec((1, tm, CONV_WIDTH), lambda bi, i: (bi, i, 0)),
    ]
    out_shape = [
        jax.ShapeDtypeStruct((b, NA_PAIRS, s, LANES), BF16),
        jax.ShapeDtypeStruct((b, DIFF_HEADS, s, LANES), BF16),
        jax.ShapeDtypeStruct((b, NA_PAIRS, s, LANES), BF16),
        jax.ShapeDtypeStruct((b, DIFF_HEADS, s, LANES), BF16),
        jax.ShapeDtypeStruct((b, NA_PAIRS, LANES, s), BF16),
        jax.ShapeDtypeStruct((b, DIFF_HEADS, DIFF_V_DIM, s), BF16),
        jax.ShapeDtypeStruct((b, s, D_MODEL), F32),
        jax.ShapeDtypeStruct((b, s, CONV_WIDTH), F32),
    ]
    return in_specs, args, out_specs, out_shape


def _inproj(x, ss, g_pre, w, layer, tables, tm):
    b, s, _ = x.shape
    in_specs, args, out_specs, out_shape = _inproj_io(b, s, ss, g_pre, w, layer, tables, tm)
    return pl.pallas_call(
        functools.partial(_inproj_kernel, rope=tables is not None),
        grid=(b, s // tm),
        in_specs=[pl.BlockSpec((1, tm, D_MODEL), lambda bi, i: (bi, i, 0))] + in_specs,
        out_specs=out_specs,
        out_shape=out_shape,
        compiler_params=_cparams(2),
        name="inproj_lat" if tables is not None else "inproj_ctx",
    )(x, *args)


def _na_heads(groups, s_ref):
    lane = lax.broadcasted_iota(jnp.int32, (NA_QBLK, LANES), 1)
    tasks = [(g, head) for g in range(len(groups)) for head in range(NA_HEADS)]

    def masked_queries(t):
        g, head = tasks[t]
        q = groups[g][0](head // 2)
        lo = (head % 2) * NA_HEAD_DIM
        return jnp.where((lane >= lo) & (lane < lo + NA_HEAD_DIM), q, jnp.zeros_like(q))

    def score_block(t, c, qm, m):
        g, head = tasks[t]
        s = _dot_nt(groups[g][1][c][0, head // 2], qm)
        if c > 0:
            s = s + groups[g][3](head, c - 1)
        s_ref[t % 2, c * KEY_BLK:(c + 1) * KEY_BLK, :] = s
        mc = jnp.max(s, axis=0, keepdims=True)
        m[0] = mc if m[0] is None else jnp.maximum(m[0], mc)

    def exp_block(t, c, m, denom):
        e = jnp.exp2(s_ref[t % 2, c * KEY_BLK:(c + 1) * KEY_BLK, :] - m[0])
        dc = jnp.sum(e, axis=0, keepdims=True)
        denom[0] = dc if denom[0] is None else denom[0] + dc
        return e.astype(BF16)

    def head_rows(head):
        return slice((head % 2) * NA_HEAD_DIM, (head % 2 + 1) * NA_HEAD_DIM)

    def pv_block(t, c, e, acc):
        g, head = tasks[t]
        oc = _dot(groups[g][2][c][0, head // 2, head_rows(head), :], e)
        acc[0] = oc if acc[0] is None else acc[0] + oc

    n_blk = len(groups[0][1])
    m_next = [None]
    qm = masked_queries(0)
    for c in range(n_blk):
        score_block(0, c, qm, m_next)
    for t, (g, head) in enumerate(tasks):
        m, m_next = m_next, [None]
        denom, acc = [None], [None]
        if t + 1 < len(tasks):
            qm = masked_queries(t + 1)
        waiting = []
        for c in range(n_blk):
            if t + 1 < len(tasks):
                score_block(t + 1, c, qm, m_next)
            waiting.append((c, exp_block(t, c, m, denom)))
            if len(waiting) > PV_LAG:
                pv_block(t, *waiting.pop(0), acc)
        for item in waiting:
            pv_block(t, *item, acc)
        groups[g][4](head // 2, head_rows(head), acc[0] * (1.0 / denom[0]))


NA_QBLKS_PER_STEP = 2


def _na_lat_kernel(*refs):
    n = NA_QBLKS_PER_STEP
    q_ref, kc_ref = refs[0], refs[1]
    k_refs = refs[2:2 + n * NA_LOC_BLKS]
    vc_ref = refs[2 + n * NA_LOC_BLKS]
    v_refs = refs[3 + n * NA_LOC_BLKS:3 + 2 * n * NA_LOC_BLKS]
    bias_refs = refs[3 + 2 * n * NA_LOC_BLKS:3 + 2 * n * NA_LOC_BLKS + n]
    o_ref, s_ref = refs[-2], refs[-1]

    def group(a):
        cols = slice(a * NA_QBLK, (a + 1) * NA_QBLK)

        def store(pair, rows, value):
            o_ref[0, pair, rows, cols] = value

        return (lambda pair: q_ref[0, pair, cols, :],
                (kc_ref,) + tuple(k_refs[a * NA_LOC_BLKS:(a + 1) * NA_LOC_BLKS]),
                (vc_ref,) + tuple(v_refs[a * NA_LOC_BLKS:(a + 1) * NA_LOC_BLKS]),
                lambda head, c: bias_refs[a][0, head, 0, c * KEY_BLK:(c + 1) * KEY_BLK, :],
                store)

    _na_heads([group(a) for a in range(n)], s_ref)


def _na_lat(qn, kn_c, kn, vnt_c, vnt, bias, layer):
    b, _, t, _ = qn.shape
    n = NA_QBLKS_PER_STEP
    n_qblk = t // NA_QBLK
    last_start = t // KEY_BLK - NA_LOC_BLKS

    def start(j):
        return jnp.clip(j - 1, 0, last_start)

    def pattern(j):
        return jnp.where(j == 0, 0, jnp.where(j == n_qblk - 1, 2, 1))

    row_blk = (1, NA_PAIRS, KEY_BLK, LANES)
    col_blk = (1, NA_PAIRS, LANES, KEY_BLK)
    k_specs = [pl.BlockSpec(row_blk, functools.partial(lambda bi, j, a, i: (bi, 0, start(n * j + a) + i, 0), a=a, i=i))
               for a in range(n) for i in range(NA_LOC_BLKS)]
    v_specs = [pl.BlockSpec(col_blk, functools.partial(lambda bi, j, a, i: (bi, 0, 0, start(n * j + a) + i), a=a, i=i))
               for a in range(n) for i in range(NA_LOC_BLKS)]
    bias_specs = [pl.BlockSpec((1, NA_HEADS, 1, NA_LOC_BLKS * KEY_BLK, NA_QBLK),
                               functools.partial(lambda bi, j, a: (layer, 0, pattern(n * j + a), 0, 0), a=a))
                  for a in range(n)]
    return pl.pallas_call(
        _na_lat_kernel,
        grid=(b, n_qblk // n),
        in_specs=[pl.BlockSpec((1, NA_PAIRS, n * NA_QBLK, LANES), lambda bi, j: (bi, 0, j, 0)),
                  pl.BlockSpec(row_blk, lambda bi, j: (bi, 0, 0, 0))] + k_specs
                 + [pl.BlockSpec(col_blk, lambda bi, j: (bi, 0, 0, 0))] + v_specs + bias_specs,
        out_specs=pl.BlockSpec((1, NA_PAIRS, LANES, n * NA_QBLK), lambda bi, j: (bi, 0, 0, j)),
        out_shape=jax.ShapeDtypeStruct((b, NA_PAIRS, LANES, t), F32),
        scratch_shapes=[pltpu.VMEM((2, (1 + NA_LOC_BLKS) * KEY_BLK, NA_QBLK), F32)],
        compiler_params=_cparams(2),
        name="na_lat",
    )(qn, kn_c, *([kn] * (n * NA_LOC_BLKS)), vnt_c, *([vnt] * (n * NA_LOC_BLKS)), *([bias] * n))


DIFF_SUB = 256
DIFF_KEY_BLK = 256


def _diff_body(q_ref, block_rows, key_blocks, val_blocks, lam_ref, g_ref, o_ref, s_ref, lam_init):
    n_sub = q_ref.shape[2] // DIFF_SUB
    edges = np.cumsum([0] + list(block_rows))
    rows = [slice(int(a), int(b)) for a, b in zip(edges[:-1], edges[1:])]
    lp = lam_ref[...]
    lam = (jnp.exp(jnp.sum(lp[0:1] * lp[1:2], axis=1, keepdims=True))
           - jnp.exp(jnp.sum(lp[2:3] * lp[3:4], axis=1, keepdims=True)) + lam_init)
    gain = g_ref[...] * (1.0 - lam_init)

    def masked_queries(i):
        q = q_ref[0, 0, i * DIFF_SUB:(i + 1) * DIFF_SUB, :]
        lane = lax.broadcasted_iota(jnp.int32, q.shape, 1)
        zero = jnp.zeros_like(q)
        return (jnp.where(lane < DIFF_QK_DIM, q, zero),
                jnp.where((lane >= DIFF_QK_DIM) & (lane < 2 * DIFF_QK_DIM), q, zero))

    def score_block(i, c, qm, m):
        k = key_blocks[c]()
        for mp in range(2):
            s = _dot_nt(k, qm[mp])
            s_ref[i % 2, mp, rows[c], :] = s
            mc = jnp.max(s, axis=0, keepdims=True)
            m[mp] = mc if m[mp] is None else jnp.maximum(m[mp], mc)

    def exp_block(i, c, m, l):
        ps = []
        for mp in range(2):
            p = jnp.exp2(s_ref[i % 2, mp, rows[c], :] - m[mp])
            lc = jnp.sum(p, axis=0, keepdims=True)
            l[mp] = lc if l[mp] is None else l[mp] + lc
            ps.append(p.astype(BF16))
        return ps

    def pv_block(c, ps, acc):
        vt = val_blocks[c]()
        for mp in range(2):
            pv = _dot(vt, ps[mp])
            acc[mp] = pv if acc[mp] is None else acc[mp] + pv

    n_blk = len(key_blocks)
    m_next = [None, None]
    qm = masked_queries(0)
    for c in range(n_blk):
        score_block(0, c, qm, m_next)
    for i in range(n_sub):
        m, m_next = m_next, [None, None]
        l, acc = [None, None], [None, None]
        if i + 1 < n_sub:
            qm = masked_queries(i + 1)
        waiting = []
        for c in range(n_blk):
            if i + 1 < n_sub:
                score_block(i + 1, c, qm, m_next)
            waiting.append((c, exp_block(i, c, m, l)))
            if len(waiting) > PV_LAG:
                pv_block(*waiting.pop(0), acc)
        for item in waiting:
            pv_block(*item, acc)
        out = acc[0] * (1.0 / l[0]) - acc[1] * (lam / l[1])
        ms = jnp.mean(out * out, axis=0, keepdims=True)
        o_ref[0, 0, :, i * DIFF_SUB:(i + 1) * DIFF_SUB] = out * lax.rsqrt(ms + EPS) * gain


def _diff_lat_kernel(q_ref, kc_ref, k_ref, vc_ref, v_ref, lam_ref, g_ref, o_ref, s_ref, *, lam_init):
    n_lat = k_ref.shape[2] // DIFF_KEY_BLK
    blk = DIFF_KEY_BLK
    keys = [lambda: kc_ref[0, 0]] + [
        functools.partial(lambda c: k_ref[0, 0, c * blk:(c + 1) * blk, :], c) for c in range(n_lat)]
    vals = [lambda: vc_ref[0, 0]] + [
        functools.partial(lambda c: v_ref[0, 0, :, c * blk:(c + 1) * blk], c) for c in range(n_lat)]
    _diff_body(q_ref, [kc_ref.shape[2]] + [blk] * n_lat, keys, vals, lam_ref, g_ref, o_ref, s_ref, lam_init)


def _diff_ctx_kernel(q_ref, kc_ref, vc_ref, lam_ref, g_ref, o_ref, s_ref, *, lam_init):
    _diff_body(q_ref, [kc_ref.shape[2]], [lambda: kc_ref[0, 0]], [lambda: vc_ref[0, 0]], lam_ref, g_ref, o_ref,
               s_ref, lam_init)


def _diff_scratch(n_keys):
    return [pltpu.VMEM((2, 2, n_keys, DIFF_SUB), F32)]


def _diff_lat(qd, kd_c, kd, vdt_c, vdt, lam_p, g_col, lam_init, qb):
    b, _, t, _ = qd.shape
    c = kd_c.shape[2]
    one_head = lambda shape, idx: pl.BlockSpec((1, 1) + shape, idx)
    return pl.pallas_call(
        functools.partial(_diff_lat_kernel, lam_init=lam_init),
        grid=(b, DIFF_HEADS, t // qb),
        in_specs=[one_head((qb, LANES), lambda bi, h, j: (bi, h, j, 0)),
                  one_head((c, LANES), lambda bi, h, j: (bi, h, 0, 0)),
                  one_head((t, LANES), lambda bi, h, j: (bi, h, 0, 0)),
                  one_head((DIFF_V_DIM, c), lambda bi, h, j: (bi, h, 0, 0)),
                  one_head((DIFF_V_DIM, t), lambda bi, h, j: (bi, h, 0, 0)),
                  pl.BlockSpec(lam_p.shape, lambda bi, h, j: (0, 0)),
                  pl.BlockSpec(g_col.shape, lambda bi, h, j: (0, 0))],
        out_specs=one_head((DIFF_V_DIM, qb), lambda bi, h, j: (bi, h, 0, j)),
        out_shape=jax.ShapeDtypeStruct((b, DIFF_HEADS, DIFF_V_DIM, t), F32),
        scratch_shapes=_diff_scratch(c + t),
        compiler_params=_cparams(3),
        name="diff_lat",
    )(qd, kd_c, kd, vdt_c, vdt, lam_p, g_col)


CONV_PAD = 16
CONV_TILE = 256
SUBLANES = 8


def _conv_kernel(u_ref, w_ref, b_ref, g_ref, beta_ref, o_ref, sh_ref):
    s = u_ref.shape[1]
    zeros = jnp.zeros((CONV_PAD, CONV_WIDTH), F32)
    sh_ref[0, 0:CONV_PAD, :] = zeros
    sh_ref[0, CONV_PAD + s:CONV_PAD + s + CONV_PAD, :] = zeros
    sh_ref[0, CONV_PAD:CONV_PAD + s, :] = u_ref[0]
    first = CONV_PAD - CONV_TAPS // 2
    n_rows = s + 2 * CONV_PAD - SUBLANES
    for r in range(1, SUBLANES):
        sh_ref[r, 0:n_rows, :] = sh_ref[0, r:r + n_rows, :]
    for t0 in range(0, s, CONV_TILE):
        acc = jnp.broadcast_to(b_ref[...], (CONV_TILE, CONV_WIDTH))
        for j in range(CONV_TAPS):
            r = (first + j) % SUBLANES
            base = t0 + first + j - r
            acc = acc + sh_ref[r, base:base + CONV_TILE, :] * w_ref[j:j + 1, :]
        mu = jnp.mean(acc, axis=-1, keepdims=True)
        d = acc - mu
        var = jnp.mean(d * d, axis=-1, keepdims=True)
        y = d * lax.rsqrt(var + EPS) * g_ref[...] + beta_ref[...]
        o_ref[0, t0:t0 + CONV_TILE, :] = _silu(y)


def _conv(u, conv_w, conv_b, ln_g, ln_b):
    b, s, _ = u.shape
    row = lambda a: a.reshape(1, CONV_WIDTH)
    vec_spec = pl.BlockSpec((1, CONV_WIDTH), lambda bi: (0, 0))
    return pl.pallas_call(
        _conv_kernel,
        grid=(b,),
        in_specs=[pl.BlockSpec((1, s, CONV_WIDTH), lambda bi: (bi, 0, 0)),
                  pl.BlockSpec((CONV_TAPS, CONV_WIDTH), lambda bi: (0, 0)),
                  vec_spec, vec_spec, vec_spec],
        out_specs=pl.BlockSpec((1, s, CONV_WIDTH), lambda bi: (bi, 0, 0)),
        out_shape=jax.ShapeDtypeStruct((b, s, CONV_WIDTH), F32),
        scratch_shapes=[pltpu.VMEM((SUBLANES, s + 2 * CONV_PAD, CONV_WIDTH), F32)],
        compiler_params=_cparams(1),
        name="conformer_conv",
    )(u, conv_w, row(conv_b), row(ln_g), row(ln_b))


def _outproj_kernel(ont_ref, odt_ref, cv_ref, gate_ref, x_ref, w_ref, gpost_ref, gmod_ref, o_ref):
    tm = x_ref.shape[1]
    a = ont_ref[0].reshape(NA_WIDTH, tm).T
    bq = odt_ref[0].reshape(DIFF_WIDTH, tm).T
    mixed = (jnp.concatenate([a, bq, cv_ref[0]], axis=1) * gate_ref[0]).astype(BF16)
    y = _dot(mixed, w_ref[0])
    ms = jnp.mean(y * y, axis=-1, keepdims=True)
    o_ref[0] = x_ref[0] + gmod_ref[0] * (y * lax.rsqrt(ms + EPS) * gpost_ref[...])


def _outproj_io(ont, odt, cv, gate, x, w, layer, g_post, gmod, tm):
    b, s, _ = x.shape
    gm_idx = (lambda bi, i: (bi, 0, 0)) if gmod.shape[0] == b else (lambda bi, i: (0, 0, 0))
    rows = lambda width: pl.BlockSpec((1, tm, width), lambda bi, i: (bi, i, 0))
    in_specs = [pl.BlockSpec((1, NA_PAIRS, LANES, tm), lambda bi, i: (bi, 0, 0, i)),
                pl.BlockSpec((1, DIFF_HEADS, DIFF_V_DIM, tm), lambda bi, i: (bi, 0, 0, i)),
                rows(CONV_WIDTH), rows(D_MODEL), rows(D_MODEL),
                pl.BlockSpec((1, D_MODEL, D_MODEL), lambda bi, i: (layer, 0, 0)),
                pl.BlockSpec((1, D_MODEL), lambda bi, i: (0, 0)),
                pl.BlockSpec((1, 1, D_MODEL), gm_idx)]
    args = [ont, odt, cv, gate, x, w, g_post.reshape(1, D_MODEL), gmod]
    return in_specs, args, rows(D_MODEL), jax.ShapeDtypeStruct((b, s, D_MODEL), F32)


def _outproj(ont, odt, cv, gate, x, w, layer, g_post, gmod, tm):
    in_specs, args, out_spec, out_shape = _outproj_io(ont, odt, cv, gate, x, w, layer, g_post, gmod, tm)
    return pl.pallas_call(
        _outproj_kernel,
        grid=(x.shape[0], x.shape[1] // tm),
        in_specs=in_specs,
        out_specs=out_spec,
        out_shape=out_shape,
        compiler_params=_cparams(2),
        name="outproj",
    )(*args)


def _outin_kernel(*refs, rope):
    n_out_in = 8
    n_in_in = 5 if rope else 3
    out_in = refs[:n_out_in]
    in_in = refs[n_out_in:n_out_in + n_in_in]
    x_new_ref = refs[n_out_in + n_in_in]
    in_out = refs[n_out_in + n_in_in + 1:]
    _outproj_kernel(*out_in, x_new_ref)
    _inproj_kernel(x_new_ref, *in_in, *in_out, rope=rope)


def _outproj_inproj(out_args, in_args, tm):
    x = out_args[4]
    b, s, _ = x.shape
    o_in_specs, o_args, o_out_spec, o_out_shape = _outproj_io(*out_args, tm)
    i_in_specs, i_args, i_out_specs, i_out_shape = _inproj_io(b, s, *in_args, tm)
    rope = in_args[-1] is not None
    return pl.pallas_call(
        functools.partial(_outin_kernel, rope=rope),
        grid=(b, s // tm),
        in_specs=o_in_specs + i_in_specs,
        out_specs=[o_out_spec] + i_out_specs,
        out_shape=[o_out_shape] + i_out_shape,
        compiler_params=_cparams(2),
        name="outin_lat" if rope else "outin_ctx",
    )(*o_args, *i_args)


def _ctx_layer_kernel(qn_ref, kn_ref, vnt_ref, qd_ref, kd_ref, vdt_ref, lam_ref, g_ref,
                      u_ref, cw_ref, cb_ref, cg_ref, cbeta_ref, gate_ref, x_ref, wout_ref, gpost_ref, gmod_ref,
                      ss_ref, gpre_ref, win_ref, xnew_ref, *rest, lam_init):
    in_out = rest[:8]
    na_s, diff_s, sh_ref, ont_s, odt_s, cv_s = rest[8:]

    def store(pair, rows, value):
        ont_s[0, pair, rows, :] = value

    _na_heads([(lambda pair: qn_ref[0, pair], (kn_ref,), (vnt_ref,), None, store)], na_s)
    for h in range(DIFF_HEADS):
        head = lambda ref: ref.at[:, h:h + 1]
        _diff_ctx_kernel(head(qd_ref), head(kd_ref), head(vdt_ref), lam_ref, g_ref, head(odt_s), diff_s.at[h],
                         lam_init=lam_init)
    _conv_kernel(u_ref, cw_ref, cb_ref, cg_ref, cbeta_ref, cv_s, sh_ref)
    _outproj_kernel(ont_s, odt_s, cv_s, gate_ref, x_ref, wout_ref, gpost_ref, gmod_ref, xnew_ref)
    _inproj_kernel(xnew_ref, ss_ref, gpre_ref, win_ref, *in_out, rope=False)


def _ctx_layer(cx, xc, lam_p, g_col, lam_init, conv_p, w_out, layer, g_post, gmod, in_args):
    kn, kd, qn, qd, vnt, vdt, gate, u = cx
    b, c, _ = xc.shape
    whole = lambda a: pl.BlockSpec((1,) + a.shape[1:], lambda bi, i: (bi,) + (0,) * (a.ndim - 1))
    const = lambda a: pl.BlockSpec(a.shape, lambda bi, i: (0,) * a.ndim)
    conv_w, conv_b, ln_g, ln_b = conv_p
    row = lambda v: v.reshape(1, -1)
    small = [lam_p, g_col]
    conv_args = [conv_w, row(conv_b), row(ln_g), row(ln_b)]
    i_in_specs, i_args, i_out_specs, i_out_shape = _inproj_io(b, c, *in_args, c)
    args = ([qn, kn, vnt, qd, kd, vdt] + small + [u] + conv_args + [gate, xc, w_out, row(g_post), gmod] + i_args)
    in_specs = ([whole(a) for a in (qn, kn, vnt, qd, kd, vdt)] + [const(a) for a in small] + [whole(u)]
                + [const(a) for a in conv_args] + [whole(gate), whole(xc),
                                                   pl.BlockSpec((1, D_MODEL, D_MODEL), lambda bi, i: (layer, 0, 0)),
                                                   const(row(g_post)), const(gmod)] + i_in_specs)
    scratch = [pltpu.VMEM((2, c, c), F32),
               pltpu.VMEM((DIFF_HEADS, 2, 2, c, DIFF_SUB), F32),
               pltpu.VMEM((SUBLANES, c + 2 * CONV_PAD, CONV_WIDTH), F32),
               pltpu.VMEM((1, NA_PAIRS, LANES, c), F32),
               pltpu.VMEM((1, DIFF_HEADS, DIFF_V_DIM, c), F32),
               pltpu.VMEM((1, c, CONV_WIDTH), F32)]
    return pl.pallas_call(
        functools.partial(_ctx_layer_kernel, lam_init=lam_init),
        grid=(b, 1),
        in_specs=in_specs,
        out_specs=[whole(xc)] + i_out_specs,
        out_shape=[jax.ShapeDtypeStruct(xc.shape, F32)] + i_out_shape,
        scratch_shapes=scratch,
        compiler_params=_cparams(2),
        name="ctx_layer",
    )(*args)


def _relayout_w_in(w_in):
    edges = np.cumsum((0, NA_WIDTH, NA_WIDTH, DIFF_WIDTH, DIFF_WIDTH, NA_WIDTH, NA_WIDTH, DIFF_WIDTH,
                       DIFF_WIDTH, CONV_WIDTH, CONV_WIDTH, CONV_WIDTH))
    na_k, na_v, df_k, df_v, na_q, na_g, df_q, df_g, cv_a, cv_b, cv_g = [
        w_in[..., a:b] for a, b in zip(edges[:-1], edges[1:])]

    def pad_heads(w):
        w = w.reshape(w.shape[:-1] + (DIFF_HEADS, DIFF_V_DIM))
        w = jnp.pad(w, ((0, 0),) * (w.ndim - 1) + ((0, LANES - DIFF_V_DIM),))
        return w.reshape(w.shape[:-2] + (DIFF_HEADS * LANES,))

    return jnp.concatenate([na_k, pad_heads(df_k), na_q, pad_heads(df_q), na_v, df_v,
                            na_g, df_g, cv_g, cv_a, cv_b], axis=-1).astype(BF16)


PROJ_TM = 512
DIFF_QB = 2048


def kernel(x, c, ctx, c_ctx, w_mod, b_mod, g_pre, g_post, w_in, w_out, na_rpb, diff_lq1, diff_lk1,
           diff_lq2, diff_lk2, diff_subln_g, conv_w, conv_b, conv_ln_g, conv_ln_b):
    batch, seq, _ = x.shape
    pad_rows = (-(batch + 1)) % 8
    cc = jnp.concatenate([c, c_ctx[None, :], jnp.zeros((pad_rows, D_MODEL), F32)], axis=0)
    mod = _modulation(cc, w_mod, b_mod)
    bias = _na_bias_tables(na_rpb.reshape(-1))
    w_in_b = _relayout_w_in(w_in)
    w_out_b = w_out.astype(BF16)
    tables = _rope_tables(seq)
    lam_p = jnp.stack([diff_lq1, diff_lk1, diff_lq2, diff_lk2], axis=1)

    xc = ctx
    ctx_tm = xc.shape[1]

    def in_args(l, latent):
        ss = mod[l, :batch, :2 * D_MODEL].reshape(batch, 1, 1, 2 * D_MODEL) if latent else \
            mod[l, batch, :2 * D_MODEL].reshape(1, 1, 1, 2 * D_MODEL)
        return ss, g_pre[l], w_in_b, l, tables if latent else None

    lat = _inproj(x, *in_args(0, True), PROJ_TM)
    cx = _inproj(xc, *in_args(0, False), ctx_tm)
    for l in range(DEPTH):
        last = l == DEPTH - 1
        lam_init = 0.8 - 0.6 * math.exp(-0.3 * l)
        g_col = diff_subln_g[l].reshape(DIFF_V_DIM, 1)
        kn, kd, qn, qd, vnt, vdt, gate, u = lat
        kn_c, kd_c, qn_c, qd_c, vnt_c, vdt_c, gate_c, u_c = cx

        ont = _na_lat(qn, kn_c, kn, vnt_c, vnt, bias, l)
        odt = _diff_lat(qd, kd_c, kd, vdt_c, vdt, lam_p[l], g_col, lam_init, DIFF_QB)
        cv = _conv(u, conv_w[l], conv_b[l], conv_ln_g[l], conv_ln_b[l])
        gmod = mod[l, :batch, 2 * D_MODEL:].reshape(batch, 1, D_MODEL)
        out_args = (ont, odt, cv, gate, x, w_out_b, l, g_post[l], gmod)
        if last:
            return _outproj(*out_args, PROJ_TM)

        gmod_c = mod[l, batch, 2 * D_MODEL:].reshape(1, 1, D_MODEL)
        conv_p = (conv_w[l], conv_b[l], conv_ln_g[l], conv_ln_b[l])
        xc, *cx_next = _ctx_layer(cx, xc, lam_p[l], g_col, lam_init, conv_p, w_out_b, l, g_post[l], gmod_c,
                                  in_args(l + 1, False))
        x, *lat = _outproj_inproj(out_args, in_args(l + 1, True), PROJ_TM)
        cx = cx_next
```

```python
import functools
import math

import numpy as np
import jax
import jax.numpy as jnp
from jax import lax
from jax.experimental import pallas as pl
from jax.experimental.pallas import tpu as pltpu

F32 = jnp.float32
BF16 = jnp.bfloat16

D_MODEL = 1024
DEPTH = 4
GRID_W = 64
NA_HEADS = 6
NA_HEAD_DIM = 64
NA_WIDTH = NA_HEADS * NA_HEAD_DIM
NA_WIN_H = 8
NA_WIN_W = 16
DIFF_HEADS = 4
DIFF_QK_DIM = 48
DIFF_V_DIM = 2 * DIFF_QK_DIM
DIFF_WIDTH = DIFF_HEADS * DIFF_V_DIM
CONV_WIDTH = D_MODEL - NA_WIDTH - DIFF_WIDTH
CONV_TAPS = 31
ROPE_BASE = 10000.0
EPS = 1e-6
NEG_INF = -1e30
LOG2_E = math.log2(math.e)
DIFF_Q_SCALE = DIFF_QK_DIM ** -0.5 * LOG2_E
NA_Q_SCALE = NA_HEAD_DIM ** -0.5 * LOG2_E

LANES = 128
NA_PAIRS = NA_HEADS // 2
KEY_BLK = 256
NA_QBLK = 256
NA_ROWS_PER_QBLK = NA_QBLK // GRID_W
NA_LOC_BLKS = 3
NA_PATTERNS = 3
VMEM_LIMIT = 56 * 1024 * 1024

IN_COLS = 4 * NA_WIDTH + 4 * DIFF_WIDTH + 3 * CONV_WIDTH


def _cparams(n_axes, vmem=VMEM_LIMIT):
    return pltpu.CompilerParams(dimension_semantics=("arbitrary",) * n_axes, vmem_limit_bytes=vmem)


def _silu(z):
    return z * jax.nn.sigmoid(z)


def _dot(a, b):
    return jnp.dot(a, b, preferred_element_type=F32)


def _dot_nt(a, b):
    return lax.dot_general(a, b, (((1,), (1,)), ((), ())), preferred_element_type=F32)


PV_LAG = 2


def _mod_kernel(cc_ref, w_ref, b_ref, o_ref):
    s = _silu(cc_ref[...])
    o_ref[0] = _dot(s, w_ref[0]) + b_ref[0]


def _modulation(cc, w_mod, b_mod):
    n_rows = cc.shape[0]
    return pl.pallas_call(
        _mod_kernel,
        grid=(DEPTH, 3),
        in_specs=[
            pl.BlockSpec((n_rows, D_MODEL), lambda l, j: (0, 0)),
            pl.BlockSpec((1, D_MODEL, D_MODEL), lambda l, j: (l, 0, j)),
            pl.BlockSpec((1, 1, D_MODEL), lambda l, j: (l, 0, j)),
        ],
        out_specs=pl.BlockSpec((1, n_rows, D_MODEL), lambda l, j: (l, 0, j)),
        out_shape=jax.ShapeDtypeStruct((DEPTH, n_rows, 3 * D_MODEL), F32),
        compiler_params=_cparams(2),
        name="adaln_mod",
    )(cc, w_mod, b_mod.reshape(DEPTH, 1, 3 * D_MODEL))


def _na_bias_kernel(rpb_ref, o_ref):
    l = pl.program_id(0)
    h = pl.program_id(1)
    n_dr = 2 * NA_WIN_H - 1
    n_dc = 2 * NA_WIN_W - 1
    base = (l * NA_HEADS + h) * (n_dr * n_dc)
    kc = lax.broadcasted_iota(jnp.int32, (GRID_W, LANES), 0)
    lane = lax.broadcasted_iota(jnp.int32, (GRID_W, LANES), 1)
    qc = lane % GRID_W
    cstart = jnp.clip(qc - NA_WIN_W // 2, 0, GRID_W - NA_WIN_W)
    col_ok = (kc >= cstart) & (kc < cstart + NA_WIN_W)
    dc = kc - qc + (NA_WIN_W - 1)
    neg = jnp.full((GRID_W, LANES), NEG_INF, F32)
    dc_masks = [col_ok & (dc == i) for i in range(n_dc)]
    tiles = []
    for dr in range(n_dr):
        t = neg
        for i in range(n_dc):
            t = jnp.where(dc_masks[i], rpb_ref[base + dr * n_dc + i], t)
        tiles.append(t * LOG2_E)
    left = lane < GRID_W

    def tile(dr, ok):
        return tiles[dr] if ok else neg

    patterns = (
        (lambda i: (0, NA_WIN_H), NA_WIN_H - 1),
        (lambda i: (i, i + NA_WIN_H), NA_WIN_H - 1 - NA_ROWS_PER_QBLK),
        (lambda i: (4, 4 + NA_WIN_H), -1),
    )
    for p, (rng, off) in enumerate(patterns):
        for j in range(NA_LOC_BLKS * KEY_BLK // GRID_W):
            halves = []
            for i0 in (0, 2):
                pair = []
                for i in (i0, i0 + 1):
                    lo, hi = rng(i)
                    pair.append(tile(j - i + off, lo <= j < hi) if 0 <= j - i + off < n_dr else neg)
                halves.append(jnp.where(left, pair[0], pair[1]))
            o_ref[0, 0, p, j * GRID_W:(j + 1) * GRID_W, :] = jnp.concatenate(halves, axis=1)


def _na_bias_tables(rpb_flat):
    n_keys = NA_LOC_BLKS * KEY_BLK
    return pl.pallas_call(
        _na_bias_kernel,
        grid=(DEPTH, NA_HEADS),
        in_specs=[pl.BlockSpec(memory_space=pltpu.SMEM)],
        out_specs=pl.BlockSpec((1, 1, NA_PATTERNS, n_keys, NA_QBLK), lambda l, h: (l, h, 0, 0, 0)),
        out_shape=jax.ShapeDtypeStruct((DEPTH, NA_HEADS, NA_PATTERNS, n_keys, NA_QBLK), F32),
        compiler_params=_cparams(2),
        name="na_bias_tables",
    )(rpb_flat)


def _rope_tables(seq):
    half = DIFF_QK_DIM // 2
    t = np.arange(seq)
    pos = np.stack([t // GRID_W, t % GRID_W], axis=1).astype(np.float64)
    lane = np.arange(LANES)
    live = lane < DIFF_V_DIM
    seg = np.where(live, lane // half, 0)
    i = lane % half
    inv = ROPE_BASE ** (-(2.0 * (i % (half // 2))) / half)
    ang = pos[:, seg % 2] * inv[None, :]
    cos = np.where(live[None, :], np.cos(ang), 1.0)
    sgn = np.where(i < half // 2, -1.0, 1.0)
    sin = np.where(live[None, :], np.sin(ang) * sgn[None, :], 0.0)
    return jnp.asarray(cos, F32), jnp.asarray(sin, F32)


def _inproj_kernel(*refs, rope):
    if rope:
        x_ref, ss_ref, gpre_ref, w_ref, cos_ref, sin_ref = refs[:6]
        outs = refs[6:]
    else:
        x_ref, ss_ref, gpre_ref, w_ref = refs[:4]
        outs = refs[4:]
    kn_ref, kd_ref, qn_ref, qd_ref, vnt_ref, vdt_ref, gate_ref, u_ref = outs

    x = x_ref[0]
    ms = jnp.mean(x * x, axis=-1, keepdims=True)
    hn = x * lax.rsqrt(ms + EPS) * gpre_ref[...]
    ss = ss_ref[0, 0]
    hb = (hn * (1.0 + ss[:, D_MODEL:]) + ss[:, :D_MODEL]).astype(BF16)

    def proj(a, b):
        return _dot(hb, w_ref[0, :, a:b])

    if rope:
        cos = cos_ref[...]
        sin = sin_ref[...]
        lane = lax.broadcasted_iota(jnp.int32, cos.shape, 1)
        quarter = DIFF_QK_DIM // 4
        first = (lane % (2 * quarter)) < quarter

        def rot(zb):
            up = pltpu.roll(zb, LANES - quarter, axis=1)
            dn = pltpu.roll(zb, quarter, axis=1)
            return zb * cos + jnp.where(first, up, dn) * sin
    else:
        def rot(zb):
            return zb

    def diff_heads(z):
        blk_lane = lax.broadcasted_iota(jnp.int32, (z.shape[0], LANES), 1)
        blocks = []
        for h in range(DIFF_HEADS):
            lo = h * DIFF_V_DIM
            v0, off = lo // LANES, lo % LANES
            part = z[:, v0 * LANES:(v0 + 1) * LANES]
            head = part if off == 0 else pltpu.roll(part, LANES - off, axis=1)
            if off + DIFF_V_DIM > LANES:
                nxt = pltpu.roll(z[:, (v0 + 1) * LANES:(v0 + 2) * LANES], LANES - off, axis=1)
                head = jnp.where(blk_lane < LANES - off, head, nxt)
            blocks.append(jnp.where(blk_lane < DIFF_V_DIM, head, 0.0))
        return blocks

    def store_kn(z):
        for p in range(NA_PAIRS):
            kn_ref[0, p] = z[:, p * LANES:(p + 1) * LANES].astype(BF16)

    def store_kd(z):
        for h, zb in enumerate(diff_heads(z)):
            kd_ref[0, h] = rot(zb).astype(BF16)

    def store_qn(z):
        for p in range(NA_PAIRS):
            qn_ref[0, p] = (z[:, p * LANES:(p + 1) * LANES] * NA_Q_SCALE).astype(BF16)

    def store_qd(z):
        for h, zb in enumerate(diff_heads(z)):
            qd_ref[0, h] = (rot(zb) * DIFF_Q_SCALE).astype(BF16)

    def store_vn(z):
        zt = z.T
        for p in range(NA_PAIRS):
            vnt_ref[0, p] = zt[p * LANES:(p + 1) * LANES].astype(BF16)

    def store_vd(z):
        zt = z.T
        for h in range(DIFF_HEADS):
            vdt_ref[0, h] = zt[h * DIFF_V_DIM:(h + 1) * DIFF_V_DIM].astype(BF16)

    def store_gate(z, col):
        gate_ref[0, :, col:col + z.shape[1]] = _silu(z)

    pair = NA_WIDTH + DIFF_WIDTH
    z = proj(0, pair)
    store_kn(z[:, :NA_WIDTH])
    store_vn(z[:, NA_WIDTH:])
    z = proj(pair, 2 * pair)
    store_kd(z[:, :DIFF_WIDTH])
    store_vd(z[:, DIFF_WIDTH:])
    z = proj(2 * pair, 3 * pair)
    store_qn(z[:, :NA_WIDTH])
    store_gate(z[:, NA_WIDTH:], 0)
    z = proj(3 * pair, 4 * pair)
    store_qd(z[:, :DIFF_WIDTH])
    store_gate(z[:, DIFF_WIDTH:], NA_WIDTH)
    z = proj(4 * pair, IN_COLS)
    u_ref[0] = z[:, :CONV_WIDTH] * jax.nn.sigmoid(z[:, CONV_WIDTH:2 * CONV_WIDTH])
    store_gate(z[:, 2 * CONV_WIDTH:], NA_WIDTH + DIFF_WIDTH)


def _inproj_io(b, s, ss, g_pre, w, layer, tables, tm):
    rope = tables is not None
    ss_idx = (lambda bi, i: (bi, 0, 0, 0)) if ss.shape[0] == b else (lambda bi, i: (0, 0, 0, 0))
    in_specs = [
        pl.BlockSpec((1, 1, 1, 2 * D_MODEL), ss_idx),
        pl.BlockSpec((1, D_MODEL), lambda bi, i: (0, 0)),
        pl.BlockSpec((1, D_MODEL, IN_COLS), lambda bi, i: (layer, 0, 0)),
    ]
    args = [ss, g_pre.reshape(1, D_MODEL), w]
    if rope:
        in_specs += [pl.BlockSpec((tm, LANES), lambda bi, i: (i, 0))] * 2
        args += list(tables)
    row_major = lambda n: pl.BlockSpec((1, n, tm, LANES), lambda bi, i: (bi, 0, i, 0))
    out_specs = [
        row_major(NA_PAIRS), row_major(DIFF_HEADS), row_major(NA_PAIRS), row_major(DIFF_HEADS),
        pl.BlockSpec((1, NA_PAIRS, LANES, tm), lambda bi, i: (bi, 0, 0, i)),
        pl.BlockSpec((1, DIFF_HEADS, DIFF_V_DIM, tm), lambda bi, i: (bi, 0, 0, i)),
        pl.BlockSpec((1, tm, D_MODEL), lambda bi, i: (bi, i, 0)),
        pl.BlockSpec((1, tm, CONV_WIDTH), lambda bi, i: (bi, i, 0)),
    ]
    out_shape = [
        jax.ShapeDtypeStruct((b, NA_PAIRS, s, LANES), BF16),
        jax.ShapeDtypeStruct((b, DIFF_HEADS, s, LANES), BF16),
        jax.ShapeDtypeStruct((b, NA_PAIRS, s, LANES), BF16),
        jax.ShapeDtypeStruct((b, DIFF_HEADS, s, LANES), BF16),
        jax.ShapeDtypeStruct((b, NA_PAIRS, LANES, s), BF16),
        jax.ShapeDtypeStruct((b, DIFF_HEADS, DIFF_V_DIM, s), BF16),
        jax.ShapeDtypeStruct((b, s, D_MODEL), F32),
        jax.ShapeDtypeStruct((b, s, CONV_WIDTH), F32),
    ]
    return in_specs, args, out_specs, out_shape


def _inproj(x, ss, g_pre, w, layer, tables, tm):
    b, s, _ = x.shape
    in_specs, args, out_specs, out_shape = _inproj_io(b, s, ss, g_pre, w, layer, tables, tm)
    return pl.pallas_call(
        functools.partial(_inproj_kernel, rope=tables is not None),
        grid=(b, s // tm),
        in_specs=[pl.BlockSpec((1, tm, D_MODEL), lambda bi, i: (bi, i, 0))] + in_specs,
        out_specs=out_specs,
        out_shape=out_shape,
        compiler_params=_cparams(2),
        name="inproj_lat" if tables is not None else "inproj_ctx",
    )(x, *args)


def _na_heads(groups, s_ref):
    lane = lax.broadcasted_iota(jnp.int32, (NA_QBLK, LANES), 1)
    tasks = [(g, head) for g in range(len(groups)) for head in range(NA_HEADS)]

    def masked_queries(t):
        g, head = tasks[t]
        q = groups[g][0](head // 2)
        lo = (head % 2) * NA_HEAD_DIM
        return jnp.where((lane >= lo) & (lane < lo + NA_HEAD_DIM), q, jnp.zeros_like(q))

    def score_block(t, c, qm, m):
        g, head = tasks[t]
        s = _dot_nt(groups[g][1][c][0, head // 2], qm)
        if c > 0:
            s = s + groups[g][3](head, c - 1)
        s_ref[t % 2, c * KEY_BLK:(c + 1) * KEY_BLK, :] = s
        mc = jnp.max(s, axis=0, keepdims=True)
        m[0] = mc if m[0] is None else jnp.maximum(m[0], mc)

    def exp_block(t, c, m, denom):
        e = jnp.exp2(s_ref[t % 2, c * KEY_BLK:(c + 1) * KEY_BLK, :] - m[0])
        dc = jnp.sum(e, axis=0, keepdims=True)
        denom[0] = dc if denom[0] is None else denom[0] + dc
        return e.astype(BF16)

    def head_rows(head):
        return slice((head % 2) * NA_HEAD_DIM, (head % 2 + 1) * NA_HEAD_DIM)

    def pv_block(t, c, e, acc):
        g, head = tasks[t]
        oc = _dot(groups[g][2][c][0, head // 2, head_rows(head), :], e)
        acc[0] = oc if acc[0] is None else acc[0] + oc

    n_blk = len(groups[0][1])
    m_next = [None]
    qm = masked_queries(0)
    for c in range(n_blk):
        score_block(0, c, qm, m_next)
    for t, (g, head) in enumerate(tasks):
        m, m_next = m_next, [None]
        denom, acc = [None], [None]
        if t + 1 < len(tasks):
            qm = masked_queries(t + 1)
        waiting = []
        for c in range(n_blk):
            if t + 1 < len(tasks):
                score_block(t + 1, c, qm, m_next)
            waiting.append((c, exp_block(t, c, m, denom)))
            if len(waiting) > PV_LAG:
                pv_block(t, *waiting.pop(0), acc)
        for item in waiting:
            pv_block(t, *item, acc)
        groups[g][4](head // 2, head_rows(head), acc[0] * (1.0 / denom[0]))


NA_QBLKS_PER_STEP = 2


def _na_lat_kernel(*refs):
    n = NA_QBLKS_PER_STEP
    q_ref, kc_ref = refs[0], refs[1]
    k_refs = refs[2:2 + n * NA_LOC_BLKS]
    vc_ref = refs[2 + n * NA_LOC_BLKS]
    v_refs = refs[3 + n * NA_LOC_BLKS:3 + 2 * n * NA_LOC_BLKS]
    bias_refs = refs[3 + 2 * n * NA_LOC_BLKS:3 + 2 * n * NA_LOC_BLKS + n]
    o_ref, s_ref = refs[-2], refs[-1]

    def group(a):
        cols = slice(a * NA_QBLK, (a + 1) * NA_QBLK)

        def store(pair, rows, value):
            o_ref[0, pair, rows, cols] = value

        return (lambda pair: q_ref[0, pair, cols, :],
                (kc_ref,) + tuple(k_refs[a * NA_LOC_BLKS:(a + 1) * NA_LOC_BLKS]),
                (vc_ref,) + tuple(v_refs[a * NA_LOC_BLKS:(a + 1) * NA_LOC_BLKS]),
                lambda head, c: bias_refs[a][0, head, 0, c * KEY_BLK:(c + 1) * KEY_BLK, :],
                store)

    _na_heads([group(a) for a in range(n)], s_ref)


def _na_lat(qn, kn_c, kn, vnt_c, vnt, bias, layer):
    b, _, t, _ = qn.shape
    n = NA_QBLKS_PER_STEP
    n_qblk = t // NA_QBLK
    last_start = t // KEY_BLK - NA_LOC_BLKS

    def start(j):
        return jnp.clip(j - 1, 0, last_start)

    def pattern(j):
        return jnp.where(j == 0, 0, jnp.where(j == n_qblk - 1, 2, 1))

    row_blk = (1, NA_PAIRS, KEY_BLK, LANES)
    col_blk = (1, NA_PAIRS, LANES, KEY_BLK)
    k_specs = [pl.BlockSpec(row_blk, functools.partial(lambda bi, j, a, i: (bi, 0, start(n * j + a) + i, 0), a=a, i=i))
               for a in range(n) for i in range(NA_LOC_BLKS)]
    v_specs = [pl.BlockSpec(col_blk, functools.partial(lambda bi, j, a, i: (bi, 0, 0, start(n * j + a) + i), a=a, i=i))
               for a in range(n) for i in range(NA_LOC_BLKS)]
    bias_specs = [pl.BlockSpec((1, NA_HEADS, 1, NA_LOC_BLKS * KEY_BLK, NA_QBLK),
                               functools.partial(lambda bi, j, a: (layer, 0, pattern(n * j + a), 0, 0), a=a))
                  for a in range(n)]
    return pl.pallas_call(
        _na_lat_kernel,
        grid=(b, n_qblk // n),
        in_specs=[pl.BlockSpec((1, NA_PAIRS, n * NA_QBLK, LANES), lambda bi, j: (bi, 0, j, 0)),
                  pl.BlockSpec(row_blk, lambda bi, j: (bi, 0, 0, 0))] + k_specs
                 + [pl.BlockSpec(col_blk, lambda bi, j: (bi, 0, 0, 0))] + v_specs + bias_specs,
        out_specs=pl.BlockSpec((1, NA_PAIRS, LANES, n * NA_QBLK), lambda bi, j: (bi, 0, 0, j)),
        out_shape=jax.ShapeDtypeStruct((b, NA_PAIRS, LANES, t), F32),
        scratch_shapes=[pltpu.VMEM((2, (1 + NA_LOC_BLKS) * KEY_BLK, NA_QBLK), F32)],
        compiler_params=_cparams(2),
        name="na_lat",
    )(qn, kn_c, *([kn] * (n * NA_LOC_BLKS)), vnt_c, *([vnt] * (n * NA_LOC_BLKS)), *([bias] * n))


DIFF_SUB = 256
DIFF_KEY_BLK = 256


def _diff_body(q_ref, block_rows, key_blocks, val_blocks, lam_ref, g_ref, o_ref, s_ref, lam_init):
    n_sub = q_ref.shape[2] // DIFF_SUB
    tasks = [(h, i) for h in range(q_ref.shape[1]) for i in range(n_sub)]
    edges = np.cumsum([0] + list(block_rows))
    rows = [slice(int(a), int(b)) for a, b in zip(edges[:-1], edges[1:])]
    lp = lam_ref[...]
    lam = (jnp.exp(jnp.sum(lp[0:1] * lp[1:2], axis=1, keepdims=True))
           - jnp.exp(jnp.sum(lp[2:3] * lp[3:4], axis=1, keepdims=True)) + lam_init)
    gain = g_ref[...] * (1.0 - lam_init)

    def masked_queries(t):
        h, i = tasks[t]
        q = q_ref[0, h, i * DIFF_SUB:(i + 1) * DIFF_SUB, :]
        lane = lax.broadcasted_iota(jnp.int32, q.shape, 1)
        zero = jnp.zeros_like(q)
        return (jnp.where(lane < DIFF_QK_DIM, q, zero),
                jnp.where((lane >= DIFF_QK_DIM) & (lane < 2 * DIFF_QK_DIM), q, zero))

    def score_block(t, c, qm, m):
        k = key_blocks[c](tasks[t][0])
        for mp in range(2):
            s = _dot_nt(k, qm[mp])
            s_ref[t % 2, mp, rows[c], :] = s
            mc = jnp.max(s, axis=0, keepdims=True)
            m[mp] = mc if m[mp] is None else jnp.maximum(m[mp], mc)

    def exp_block(t, c, m, l):
        ps = []
        for mp in range(2):
            p = jnp.exp2(s_ref[t % 2, mp, rows[c], :] - m[mp])
            lc = jnp.sum(p, axis=0, keepdims=True)
            l[mp] = lc if l[mp] is None else l[mp] + lc
            ps.append(p.astype(BF16))
        return ps

    def pv_block(t, c, ps, acc):
        vt = val_blocks[c](tasks[t][0])
        for mp in range(2):
            pv = _dot(vt, ps[mp])
            acc[mp] = pv if acc[mp] is None else acc[mp] + pv

    n_blk = len(key_blocks)
    m_next = [None, None]
    qm = masked_queries(0)
    for c in range(n_blk):
        score_block(0, c, qm, m_next)
    for t, (h, i) in enumerate(tasks):
        m, m_next = m_next, [None, None]
        l, acc = [None, None], [None, None]
        if t + 1 < len(tasks):
            qm = masked_queries(t + 1)
        waiting = []
        for c in range(n_blk):
            if t + 1 < len(tasks):
                score_block(t + 1, c, qm, m_next)
            waiting.append((c, exp_block(t, c, m, l)))
            if len(waiting) > PV_LAG:
                pv_block(t, *waiting.pop(0), acc)
        for item in waiting:
            pv_block(t, *item, acc)
        out = acc[0] * (1.0 / l[0]) - acc[1] * (lam / l[1])
        ms = jnp.mean(out * out, axis=0, keepdims=True)
        o_ref[0, h, :, i * DIFF_SUB:(i + 1) * DIFF_SUB] = out * lax.rsqrt(ms + EPS) * gain


def _diff_lat_kernel(q_ref, kc_ref, k_ref, vc_ref, v_ref, lam_ref, g_ref, o_ref, s_ref, *, lam_init):
    n_lat = k_ref.shape[2] // DIFF_KEY_BLK
    blk = DIFF_KEY_BLK
    keys = [lambda h: kc_ref[0, h]] + [
        functools.partial(lambda c, h: k_ref[0, h, c * blk:(c + 1) * blk, :], c) for c in range(n_lat)]
    vals = [lambda h: vc_ref[0, h]] + [
        functools.partial(lambda c, h: v_ref[0, h, :, c * blk:(c + 1) * blk], c) for c in range(n_lat)]
    _diff_body(q_ref, [kc_ref.shape[2]] + [blk] * n_lat, keys, vals, lam_ref, g_ref, o_ref, s_ref, lam_init)


def _diff_ctx_kernel(q_ref, kc_ref, vc_ref, lam_ref, g_ref, o_ref, s_ref, *, lam_init):
    _diff_body(q_ref, [kc_ref.shape[2]], [lambda h: kc_ref[0, h]], [lambda h: vc_ref[0, h]], lam_ref, g_ref,
               o_ref, s_ref, lam_init)


def _diff_scratch(n_keys):
    return [pltpu.VMEM((2, 2, n_keys, DIFF_SUB), F32)]


DIFF_HEADS_PER_STEP = 2


def _diff_lat(qd, kd_c, kd, vdt_c, vdt, lam_p, g_col, lam_init, qb):
    b, _, t, _ = qd.shape
    c = kd_c.shape[2]
    heads = lambda shape, idx: pl.BlockSpec((1, DIFF_HEADS_PER_STEP) + shape, idx)
    return pl.pallas_call(
        functools.partial(_diff_lat_kernel, lam_init=lam_init),
        grid=(b, DIFF_HEADS // DIFF_HEADS_PER_STEP, t // qb),
        in_specs=[heads((qb, LANES), lambda bi, h, j: (bi, h, j, 0)),
                  heads((c, LANES), lambda bi, h, j: (bi, h, 0, 0)),
                  heads((t, LANES), lambda bi, h, j: (bi, h, 0, 0)),
                  heads((DIFF_V_DIM, c), lambda bi, h, j: (bi, h, 0, 0)),
                  heads((DIFF_V_DIM, t), lambda bi, h, j: (bi, h, 0, 0)),
                  pl.BlockSpec(lam_p.shape, lambda bi, h, j: (0, 0)),
                  pl.BlockSpec(g_col.shape, lambda bi, h, j: (0, 0))],
        out_specs=heads((DIFF_V_DIM, qb), lambda bi, h, j: (bi, h, 0, j)),
        out_shape=jax.ShapeDtypeStruct((b, DIFF_HEADS, DIFF_V_DIM, t), F32),
        scratch_shapes=_diff_scratch(c + t),
        compiler_params=_cparams(3),
        name="diff_lat",
    )(qd, kd_c, kd, vdt_c, vdt, lam_p, g_col)


CONV_PAD = 16
CONV_TILE = 256
SUBLANES = 8


def _conv_kernel(u_ref, w_ref, b_ref, g_ref, beta_ref, o_ref, sh_ref):
    s = u_ref.shape[1]
    zeros = jnp.zeros((CONV_PAD, CONV_WIDTH), F32)
    sh_ref[0, 0:CONV_PAD, :] = zeros
    sh_ref[0, CONV_PAD + s:CONV_PAD + s + CONV_PAD, :] = zeros
    sh_ref[0, CONV_PAD:CONV_PAD + s, :] = u_ref[0]
    first = CONV_PAD - CONV_TAPS // 2
    n_rows = s + 2 * CONV_PAD - SUBLANES
    for r in range(1, SUBLANES):
        sh_ref[r, 0:n_rows, :] = sh_ref[0, r:r + n_rows, :]
    for t0 in range(0, s, CONV_TILE):
        acc = jnp.broadcast_to(b_ref[...], (CONV_TILE, CONV_WIDTH))
        for j in range(CONV_TAPS):
            r = (first + j) % SUBLANES
            base = t0 + first + j - r
            acc = acc + sh_ref[r, base:base + CONV_TILE, :] * w_ref[j:j + 1, :]
        mu = jnp.mean(acc, axis=-1, keepdims=True)
        d = acc - mu
        var = jnp.mean(d * d, axis=-1, keepdims=True)
        y = d * lax.rsqrt(var + EPS) * g_ref[...] + beta_ref[...]
        o_ref[0, t0:t0 + CONV_TILE, :] = _silu(y)


def _conv(u, conv_w, conv_b, ln_g, ln_b):
    b, s, _ = u.shape
    row = lambda a: a.reshape(1, CONV_WIDTH)
    vec_spec = pl.BlockSpec((1, CONV_WIDTH), lambda bi: (0, 0))
    return pl.pallas_call(
        _conv_kernel,
        grid=(b,),
        in_specs=[pl.BlockSpec((1, s, CONV_WIDTH), lambda bi: (bi, 0, 0)),
                  pl.BlockSpec((CONV_TAPS, CONV_WIDTH), lambda bi: (0, 0)),
                  vec_spec, vec_spec, vec_spec],
        out_specs=pl.BlockSpec((1, s, CONV_WIDTH), lambda bi: (bi, 0, 0)),
        out_shape=jax.ShapeDtypeStruct((b, s, CONV_WIDTH), F32),
        scratch_shapes=[pltpu.VMEM((SUBLANES, s + 2 * CONV_PAD, CONV_WIDTH), F32)],
        compiler_params=_cparams(1),
        name="conformer_conv",
    )(u, conv_w, row(conv_b), row(ln_g), row(ln_b))


def _outproj_kernel(ont_ref, odt_ref, cv_ref, gate_ref, x_ref, w_ref, gpost_ref, gmod_ref, o_ref):
    tm = x_ref.shape[1]
    a = ont_ref[0].reshape(NA_WIDTH, tm).T
    bq = odt_ref[0].reshape(DIFF_WIDTH, tm).T
    mixed = (jnp.concatenate([a, bq, cv_ref[0]], axis=1) * gate_ref[0]).astype(BF16)
    y = _dot(mixed, w_ref[0])
    ms = jnp.mean(y * y, axis=-1, keepdims=True)
    o_ref[0] = x_ref[0] + gmod_ref[0] * (y * lax.rsqrt(ms + EPS) * gpost_ref[...])


def _outproj_io(ont, odt, cv, gate, x, w, layer, g_post, gmod, tm):
    b, s, _ = x.shape
    gm_idx = (lambda bi, i: (bi, 0, 0)) if gmod.shape[0] == b else (lambda bi, i: (0, 0, 0))
    rows = lambda width: pl.BlockSpec((1, tm, width), lambda bi, i: (bi, i, 0))
    in_specs = [pl.BlockSpec((1, NA_PAIRS, LANES, tm), lambda bi, i: (bi, 0, 0, i)),
                pl.BlockSpec((1, DIFF_HEADS, DIFF_V_DIM, tm), lambda bi, i: (bi, 0, 0, i)),
                rows(CONV_WIDTH), rows(D_MODEL), rows(D_MODEL),
                pl.BlockSpec((1, D_MODEL, D_MODEL), lambda bi, i: (layer, 0, 0)),
                pl.BlockSpec((1, D_MODEL), lambda bi, i: (0, 0)),
                pl.BlockSpec((1, 1, D_MODEL), gm_idx)]
    args = [ont, odt, cv, gate, x, w, g_post.reshape(1, D_MODEL), gmod]
    return in_specs, args, rows(D_MODEL), jax.ShapeDtypeStruct((b, s, D_MODEL), F32)


def _outproj(ont, odt, cv, gate, x, w, layer, g_post, gmod, tm):
    in_specs, args, out_spec, out_shape = _outproj_io(ont, odt, cv, gate, x, w, layer, g_post, gmod, tm)
    return pl.pallas_call(
        _outproj_kernel,
        grid=(x.shape[0], x.shape[1] // tm),
        in_specs=in_specs,
        out_specs=out_spec,
        out_shape=out_shape,
        compiler_params=_cparams(2),
        name="outproj",
    )(*args)


def _outin_kernel(*refs, rope):
    n_out_in = 8
    n_in_in = 5 if rope else 3
    out_in = refs[:n_out_in]
    in_in = refs[n_out_in:n_out_in + n_in_in]
    x_new_ref = refs[n_out_in + n_in_in]
    in_out = refs[n_out_in + n_in_in + 1:]
    _outproj_kernel(*out_in, x_new_ref)
    _inproj_kernel(x_new_ref, *in_in, *in_out, rope=rope)


def _outproj_inproj(out_args, in_args, tm):
    x = out_args[4]
    b, s, _ = x.shape
    o_in_specs, o_args, o_out_spec, o_out_shape = _outproj_io(*out_args, tm)
    i_in_specs, i_args, i_out_specs, i_out_shape = _inproj_io(b, s, *in_args, tm)
    rope = in_args[-1] is not None
    return pl.pallas_call(
        functools.partial(_outin_kernel, rope=rope),
        grid=(b, s // tm),
        in_specs=o_in_specs + i_in_specs,
        out_specs=[o_out_spec] + i_out_specs,
        out_shape=[o_out_shape] + i_out_shape,
        compiler_params=_cparams(2),
        name="outin_lat" if rope else "outin_ctx",
    )(*o_args, *i_args)


def _ctx_layer_kernel(qn_ref, kn_ref, vnt_ref, qd_ref, kd_ref, vdt_ref, lam_ref, g_ref,
                      u_ref, cw_ref, cb_ref, cg_ref, cbeta_ref, gate_ref, x_ref, wout_ref, gpost_ref, gmod_ref,
                      ss_ref, gpre_ref, win_ref, xnew_ref, *rest, lam_init):
    in_out = rest[:8]
    na_s, diff_s, sh_ref, ont_s, odt_s, cv_s = rest[8:]

    def store(pair, rows, value):
        ont_s[0, pair, rows, :] = value

    _na_heads([(lambda pair: qn_ref[0, pair], (kn_ref,), (vnt_ref,), None, store)], na_s)
    for h in range(DIFF_HEADS):
        head = lambda ref: ref.at[:, h:h + 1]
        _diff_ctx_kernel(head(qd_ref), head(kd_ref), head(vdt_ref), lam_ref, g_ref, head(odt_s), diff_s.at[h],
                         lam_init=lam_init)
    _conv_kernel(u_ref, cw_ref, cb_ref, cg_ref, cbeta_ref, cv_s, sh_ref)
    _outproj_kernel(ont_s, odt_s, cv_s, gate_ref, x_ref, wout_ref, gpost_ref, gmod_ref, xnew_ref)
    _inproj_kernel(xnew_ref, ss_ref, gpre_ref, win_ref, *in_out, rope=False)


def _ctx_layer(cx, xc, lam_p, g_col, lam_init, conv_p, w_out, layer, g_post, gmod, in_args):
    kn, kd, qn, qd, vnt, vdt, gate, u = cx
    b, c, _ = xc.shape
    whole = lambda a: pl.BlockSpec((1,) + a.shape[1:], lambda bi, i: (bi,) + (0,) * (a.ndim - 1))
    const = lambda a: pl.BlockSpec(a.shape, lambda bi, i: (0,) * a.ndim)
    conv_w, conv_b, ln_g, ln_b = conv_p
    row = lambda v: v.reshape(1, -1)
    small = [lam_p, g_col]
    conv_args = [conv_w, row(conv_b), row(ln_g), row(ln_b)]
    i_in_specs, i_args, i_out_specs, i_out_shape = _inproj_io(b, c, *in_args, c)
    args = ([qn, kn, vnt, qd, kd, vdt] + small + [u] + conv_args + [gate, xc, w_out, row(g_post), gmod] + i_args)
    in_specs = ([whole(a) for a in (qn, kn, vnt, qd, kd, vdt)] + [const(a) for a in small] + [whole(u)]
                + [const(a) for a in conv_args] + [whole(gate), whole(xc),
                                                   pl.BlockSpec((1, D_MODEL, D_MODEL), lambda bi, i: (layer, 0, 0)),
                                                   const(row(g_post)), const(gmod)] + i_in_specs)
    scratch = [pltpu.VMEM((2, c, c), F32),
               pltpu.VMEM((DIFF_HEADS, 2, 2, c, DIFF_SUB), F32),
               pltpu.VMEM((SUBLANES, c + 2 * CONV_PAD, CONV_WIDTH), F32),
               pltpu.VMEM((1, NA_PAIRS, LANES, c), F32),
               pltpu.VMEM((1, DIFF_HEADS, DIFF_V_DIM, c), F32),
               pltpu.VMEM((1, c, CONV_WIDTH), F32)]
    return pl.pallas_call(
        functools.partial(_ctx_layer_kernel, lam_init=lam_init),
        grid=(b, 1),
        in_specs=in_specs,
        out_specs=[whole(xc)] + i_out_specs,
        out_shape=[jax.ShapeDtypeStruct(xc.shape, F32)] + i_out_shape,
        scratch_shapes=scratch,
        compiler_params=_cparams(2),
        name="ctx_layer",
    )(*args)


PROJ_TM = 512
DIFF_QB = 2048


def kernel(x, c, ctx, c_ctx, w_mod, b_mod, g_pre, g_post, w_in, w_out, na_rpb, diff_lq1, diff_lk1,
           diff_lq2, diff_lk2, diff_subln_g, conv_w, conv_b, conv_ln_g, conv_ln_b):
    batch, seq, _ = x.shape
    pad_rows = (-(batch + 1)) % 8
    cc = jnp.concatenate([c, c_ctx[None, :], jnp.zeros((pad_rows, D_MODEL), F32)], axis=0)
    mod = _modulation(cc, w_mod, b_mod)
    bias = _na_bias_tables(na_rpb.reshape(-1))
    w_in_b = w_in.astype(BF16)
    w_out_b = w_out.astype(BF16)
    tables = _rope_tables(seq)
    lam_p = jnp.stack([diff_lq1, diff_lk1, diff_lq2, diff_lk2], axis=1)

    xc = ctx
    ctx_tm = xc.shape[1]

    def in_args(l, latent):
        ss = mod[l, :batch, :2 * D_MODEL].reshape(batch, 1, 1, 2 * D_MODEL) if latent else \
            mod[l, batch, :2 * D_MODEL].reshape(1, 1, 1, 2 * D_MODEL)
        return ss, g_pre[l], w_in_b, l, tables if latent else None

    lat = _inproj(x, *in_args(0, True), PROJ_TM)
    cx = _inproj(xc, *in_args(0, False), ctx_tm)
    for l in range(DEPTH):
        last = l == DEPTH - 1
        lam_init = 0.8 - 0.6 * math.exp(-0.3 * l)
        g_col = diff_subln_g[l].reshape(DIFF_V_DIM, 1)
        kn, kd, qn, qd, vnt, vdt, gate, u = lat
        kn_c, kd_c, qn_c, qd_c, vnt_c, vdt_c, gate_c, u_c = cx

        ont = _na_lat(qn, kn_c, kn, vnt_c, vnt, bias, l)
        odt = _diff_lat(qd, kd_c, kd, vdt_c, vdt, lam_p[l], g_col, lam_init, DIFF_QB)
        cv = _conv(u, conv_w[l], conv_b[l], conv_ln_g[l], conv_ln_b[l])
        gmod = mod[l, :batch, 2 * D_MODEL:].reshape(batch, 1, D_MODEL)
        out_args = (ont, odt, cv, gate, x, w_out_b, l, g_post[l], gmod)
        if last:
            return _outproj(*out_args, PROJ_TM)

        gmod_c = mod[l, batch, 2 * D_MODEL:].reshape(1, 1, D_MODEL)
        conv_p = (conv_w[l], conv_b[l], conv_ln_g[l], conv_ln_b[l])
        xc, *cx_next = _ctx_layer(cx, xc, lam_p[l], g_col, lam_init, conv_p, w_out_b, l, g_post[l], gmod_c,
                                  in_args(l + 1, False))
        x, *lat = _outproj_inproj(out_args, in_args(l + 1, True), PROJ_TM)
        cx = cx_next
```

```python
import functools
import math

import numpy as np
import jax
import jax.numpy as jnp
from jax import lax
from jax.experimental import pallas as pl
from jax.experimental.pallas import tpu as pltpu

F32 = jnp.float32
BF16 = jnp.bfloat16

D_MODEL = 1024
DEPTH = 4
GRID_W = 64
NA_HEADS = 6
NA_HEAD_DIM = 64
NA_WIDTH = NA_HEADS * NA_HEAD_DIM
NA_WIN_H = 8
NA_WIN_W = 16
DIFF_HEADS = 4
DIFF_QK_DIM = 48
DIFF_V_DIM = 2 * DIFF_QK_DIM
DIFF_WIDTH = DIFF_HEADS * DIFF_V_DIM
CONV_WIDTH = D_MODEL - NA_WIDTH - DIFF_WIDTH
CONV_TAPS = 31
ROPE_BASE = 10000.0
EPS = 1e-6
NEG_INF = -1e30
LOG2_E = math.log2(math.e)
DIFF_Q_SCALE = DIFF_QK_DIM ** -0.5 * LOG2_E
NA_Q_SCALE = NA_HEAD_DIM ** -0.5 * LOG2_E

LANES = 128
NA_PAIRS = NA_HEADS // 2
KEY_BLK = 256
NA_QBLK = 256
NA_ROWS_PER_QBLK = NA_QBLK // GRID_W
NA_LOC_BLKS = 3
NA_PATTERNS = 3
VMEM_LIMIT = 56 * 1024 * 1024

IN_COLS = 4 * NA_WIDTH + 4 * DIFF_WIDTH + 3 * CONV_WIDTH


def _cparams(n_axes, vmem=VMEM_LIMIT):
    return pltpu.CompilerParams(dimension_semantics=("arbitrary",) * n_axes, vmem_limit_bytes=vmem)


def _silu(z):
    return z * jax.nn.sigmoid(z)


def _dot(a, b):
    return jnp.dot(a, b, preferred_element_type=F32)


def _dot_nt(a, b):
    return lax.dot_general(a, b, (((1,), (1,)), ((), ())), preferred_element_type=F32)


PV_LAG = 2


def _mod_kernel(cc_ref, w_ref, b_ref, o_ref):
    s = _silu(cc_ref[...])
    o_ref[0] = _dot(s, w_ref[0]) + b_ref[0]


def _modulation(cc, w_mod, b_mod):
    n_rows = cc.shape[0]
    return pl.pallas_call(
        _mod_kernel,
        grid=(DEPTH, 3),
        in_specs=[
            pl.BlockSpec((n_rows, D_MODEL), lambda l, j: (0, 0)),
            pl.BlockSpec((1, D_MODEL, D_MODEL), lambda l, j: (l, 0, j)),
            pl.BlockSpec((1, 1, D_MODEL), lambda l, j: (l, 0, j)),
        ],
        out_specs=pl.BlockSpec((1, n_rows, D_MODEL), lambda l, j: (l, 0, j)),
        out_shape=jax.ShapeDtypeStruct((DEPTH, n_rows, 3 * D_MODEL), F32),
        compiler_params=_cparams(2),
        name="adaln_mod",
    )(cc, w_mod, b_mod.reshape(DEPTH, 1, 3 * D_MODEL))


def _na_bias_kernel(rpb_ref, o_ref):
    l = pl.program_id(0)
    h = pl.program_id(1)
    n_dr = 2 * NA_WIN_H - 1
    n_dc = 2 * NA_WIN_W - 1
    base = (l * NA_HEADS + h) * (n_dr * n_dc)
    kc = lax.broadcasted_iota(jnp.int32, (GRID_W, LANES), 0)
    lane = lax.broadcasted_iota(jnp.int32, (GRID_W, LANES), 1)
    qc = lane % GRID_W
    cstart = jnp.clip(qc - NA_WIN_W // 2, 0, GRID_W - NA_WIN_W)
    col_ok = (kc >= cstart) & (kc < cstart + NA_WIN_W)
    dc = kc - qc + (NA_WIN_W - 1)
    neg = jnp.full((GRID_W, LANES), NEG_INF, F32)
    dc_masks = [col_ok & (dc == i) for i in range(n_dc)]
    tiles = []
    for dr in range(n_dr):
        t = neg
        for i in range(n_dc):
            t = jnp.where(dc_masks[i], rpb_ref[base + dr * n_dc + i], t)
        tiles.append(t * LOG2_E)
    left = lane < GRID_W

    def tile(dr, ok):
        return tiles[dr] if ok else neg

    patterns = (
        (lambda i: (0, NA_WIN_H), NA_WIN_H - 1),
        (lambda i: (i, i + NA_WIN_H), NA_WIN_H - 1 - NA_ROWS_PER_QBLK),
        (lambda i: (4, 4 + NA_WIN_H), -1),
    )
    for p, (rng, off) in enumerate(patterns):
        for j in range(NA_LOC_BLKS * KEY_BLK // GRID_W):
            halves = []
            for i0 in (0, 2):
                pair = []
                for i in (i0, i0 + 1):
                    lo, hi = rng(i)
                    pair.append(tile(j - i + off, lo <= j < hi) if 0 <= j - i + off < n_dr else neg)
                halves.append(jnp.where(left, pair[0], pair[1]))
            o_ref[0, 0, p, j * GRID_W:(j + 1) * GRID_W, :] = jnp.concatenate(halves, axis=1)


def _na_bias_tables(rpb_flat):
    n_keys = NA_LOC_BLKS * KEY_BLK
    return pl.pallas_call(
        _na_bias_kernel,
        grid=(DEPTH, NA_HEADS),
        in_specs=[pl.BlockSpec(memory_space=pltpu.SMEM)],
        out_specs=pl.BlockSpec((1, 1, NA_PATTERNS, n_keys, NA_QBLK), lambda l, h: (l, h, 0, 0, 0)),
        out_shape=jax.ShapeDtypeStruct((DEPTH, NA_HEADS, NA_PATTERNS, n_keys, NA_QBLK), F32),
        compiler_params=_cparams(2),
        name="na_bias_tables",
    )(rpb_flat)


def _rope_tables(seq):
    half = DIFF_QK_DIM // 2
    t = np.arange(seq)
    pos = np.stack([t // GRID_W, t % GRID_W], axis=1).astype(np.float64)
    lane = np.arange(LANES)
    live = lane < DIFF_V_DIM
    seg = np.where(live, lane // half, 0)
    i = lane % half
    inv = ROPE_BASE ** (-(2.0 * (i % (half // 2))) / half)
    ang = pos[:, seg % 2] * inv[None, :]
    cos = np.where(live[None, :], np.cos(ang), 1.0)
    sgn = np.where(i < half // 2, -1.0, 1.0)
    sin = np.where(live[None, :], np.sin(ang) * sgn[None, :], 0.0)
    return jnp.asarray(cos, F32), jnp.asarray(sin, F32)


def _inproj_kernel(*refs, rope):
    if rope:
        x_ref, ss_ref, gpre_ref, w_ref, cos_ref, sin_ref = refs[:6]
        outs = refs[6:]
    else:
        x_ref, ss_ref, gpre_ref, w_ref = refs[:4]
        outs = refs[4:]
    kn_ref, kd_ref, qn_ref, qd_ref, vnt_ref, vdt_ref, gate_ref, u_ref = outs

    x = x_ref[0]
    ms = jnp.mean(x * x, axis=-1, keepdims=True)
    hn = x * lax.rsqrt(ms + EPS) * gpre_ref[...]
    ss = ss_ref[0, 0]
    hb = (hn * (1.0 + ss[:, D_MODEL:]) + ss[:, :D_MODEL]).astype(BF16)

    def proj(a, b):
        return _dot(hb, w_ref[0, :, a:b])

    if rope:
        cos = cos_ref[...]
        sin = sin_ref[...]
        lane = lax.broadcasted_iota(jnp.int32, cos.shape, 1)
        quarter = DIFF_QK_DIM // 4
        first = (lane % (2 * quarter)) < quarter

        def rot(zb):
            up = pltpu.roll(zb, LANES - quarter, axis=1)
            dn = pltpu.roll(zb, quarter, axis=1)
            return zb * cos + jnp.where(first, up, dn) * sin
    else:
        def rot(zb):
            return zb

    def diff_heads(z):
        blk_lane = lax.broadcasted_iota(jnp.int32, (z.shape[0], LANES), 1)
        blocks = []
        for h in range(DIFF_HEADS):
            lo = h * DIFF_V_DIM
            v0, off = lo // LANES, lo % LANES
            part = z[:, v0 * LANES:(v0 + 1) * LANES]
            head = part if off == 0 else pltpu.roll(part, LANES - off, axis=1)
            if off + DIFF_V_DIM > LANES:
                nxt = pltpu.roll(z[:, (v0 + 1) * LANES:(v0 + 2) * LANES], LANES - off, axis=1)
                head = jnp.where(blk_lane < LANES - off, head, nxt)
            blocks.append(jnp.where(blk_lane < DIFF_V_DIM, head, 0.0))
        return blocks

    def store_kn(z):
        for p in range(NA_PAIRS):
            kn_ref[0, p] = z[:, p * LANES:(p + 1) * LANES].astype(BF16)

    def store_kd(z):
        for h, zb in enumerate(diff_heads(z)):
            kd_ref[0, h] = rot(zb).astype(BF16)

    def store_qn(z):
        for p in range(NA_PAIRS):
            qn_ref[0, p] = (z[:, p * LANES:(p + 1) * LANES] * NA_Q_SCALE).astype(BF16)

    def store_qd(z):
        for h, zb in enumerate(diff_heads(z)):
            qd_ref[0, h] = (rot(zb) * DIFF_Q_SCALE).astype(BF16)

    def store_vn(z):
        zt = z.T
        for p in range(NA_PAIRS):
            vnt_ref[0, p] = zt[p * LANES:(p + 1) * LANES].astype(BF16)

    def store_vd(z):
        zt = z.T
        for h in range(DIFF_HEADS):
            vdt_ref[0, h] = zt[h * DIFF_V_DIM:(h + 1) * DIFF_V_DIM].astype(BF16)

    def store_gate(z, col):
        gate_ref[0, :, col:col + z.shape[1]] = _silu(z)

    pair = NA_WIDTH + DIFF_WIDTH
    z = proj(0, pair)
    store_kn(z[:, :NA_WIDTH])
    store_vn(z[:, NA_WIDTH:])
    z = proj(pair, 2 * pair)
    store_kd(z[:, :DIFF_WIDTH])
    store_vd(z[:, DIFF_WIDTH:])
    z = proj(2 * pair, 3 * pair)
    store_qn(z[:, :NA_WIDTH])
    store_gate(z[:, NA_WIDTH:], 0)
    z = proj(3 * pair, 4 * pair)
    store_qd(z[:, :DIFF_WIDTH])
    store_gate(z[:, DIFF_WIDTH:], NA_WIDTH)
    z = proj(4 * pair, IN_COLS)
    u_ref[0] = z[:, :CONV_WIDTH] * jax.nn.sigmoid(z[:, CONV_WIDTH:2 * CONV_WIDTH])
    store_gate(z[:, 2 * CONV_WIDTH:], NA_WIDTH + DIFF_WIDTH)


def _inproj_io(b, s, ss, g_pre, w, layer, tables, tm):
    rope = tables is not None
    ss_idx = (lambda bi, i: (bi, 0, 0, 0)) if ss.shape[0] == b else (lambda bi, i: (0, 0, 0, 0))
    in_specs = [
        pl.BlockSpec((1, 1, 1, 2 * D_MODEL), ss_idx),
        pl.BlockSpec((1, D_MODEL), lambda bi, i: (0, 0)),
        pl.BlockSpec((1, D_MODEL, IN_COLS), lambda bi, i: (layer, 0, 0)),
    ]
    args = [ss, g_pre.reshape(1, D_MODEL), w]
    if rope:
        in_specs += [pl.BlockSpec((tm, LANES), lambda bi, i: (i, 0))] * 2
        args += list(tables)
    row_major = lambda n: pl.BlockSpec((1, n, tm, LANES), lambda bi, i: (bi, 0, i, 0))
    out_specs = [
        row_major(NA_PAIRS), row_major(DIFF_HEADS), row_major(NA_PAIRS), row_major(DIFF_HEADS),
        pl.BlockSpec((1, NA_PAIRS, LANES, tm), lambda bi, i: (bi, 0, 0, i)),
        pl.BlockSpec((1, DIFF_HEADS, DIFF_V_DIM, tm), lambda bi, i: (bi, 0, 0, i)),
        pl.BlockSpec((1, tm, D_MODEL), lambda bi, i: (bi, i, 0)),
        pl.BlockSpec((1, tm, CONV_WIDTH), lambda bi, i: (bi, i, 0)),
    ]
    out_shape = [
        jax.ShapeDtypeStruct((b, NA_PAIRS, s, LANES), BF16),
        jax.ShapeDtypeStruct((b, DIFF_HEADS, s, LANES), BF16),
        jax.ShapeDtypeStruct((b, NA_PAIRS, s, LANES), BF16),
        jax.ShapeDtypeStruct((b, DIFF_HEADS, s, LANES), BF16),
        jax.ShapeDtypeStruct((b, NA_PAIRS, LANES, s), BF16),
        jax.ShapeDtypeStruct((b, DIFF_HEADS, DIFF_V_DIM, s), BF16),
        jax.ShapeDtypeStruct((b, s, D_MODEL), F32),
        jax.ShapeDtypeStruct((b, s, CONV_WIDTH), F32),
    ]
    return in_specs, args, out_specs, out_shape


def _inproj(x, ss, g_pre, w, layer, tables, tm):
    b, s, _ = x.shape
    in_specs, args, out_specs, out_shape = _inproj_io(b, s, ss, g_pre, w, layer, tables, tm)
    return pl.pallas_call(
        functools.partial(_inproj_kernel, rope=tables is not None),
        grid=(b, s // tm),
        in_specs=[pl.BlockSpec((1, tm, D_MODEL), lambda bi, i: (bi, i, 0))] + in_specs,
        out_specs=out_specs,
        out_shape=out_shape,
        compiler_params=_cparams(2),
        name="inproj_lat" if tables is not None else "inproj_ctx",
    )(x, *args)


def _na_heads(groups, s_ref):
    lane = lax.broadcasted_iota(jnp.int32, (NA_QBLK, LANES), 1)
    tasks = [(g, head) for g in range(len(groups)) for head in range(NA_HEADS)]

    def masked_queries(t):
        g, head = tasks[t]
        q = groups[g][0](head // 2)
        lo = (head % 2) * NA_HEAD_DIM
        return jnp.where((lane >= lo) & (lane < lo + NA_HEAD_DIM), q, jnp.zeros_like(q))

    def score_block(t, c, qm, m):
        g, head = tasks[t]
        s = _dot_nt(groups[g][1][c][0, head // 2], qm)
        if c > 0:
            s = s + groups[g][3](head, c - 1)
        s_ref[t % 2, c * KEY_BLK:(c + 1) * KEY_BLK, :] = s
        mc = jnp.max(s, axis=0, keepdims=True)
        m[0] = mc if m[0] is None else jnp.maximum(m[0], mc)

    def exp_block(t, c, m, denom):
        e = jnp.exp2(s_ref[t % 2, c * KEY_BLK:(c + 1) * KEY_BLK, :] - m[0])
        dc = jnp.sum(e, axis=0, keepdims=True)
        denom[0] = dc if denom[0] is None else denom[0] + dc
        return e.astype(BF16)

    def head_rows(head):
        return slice((head % 2) * NA_HEAD_DIM, (head % 2 + 1) * NA_HEAD_DIM)

    def pv_block(t, c, e, acc):
        g, head = tasks[t]
        oc = _dot(groups[g][2][c][0, head // 2, head_rows(head), :], e)
        acc[0] = oc if acc[0] is None else acc[0] + oc

    n_blk = len(groups[0][1])
    m_next = [None]
    qm = masked_queries(0)
    for c in range(n_blk):
        score_block(0, c, qm, m_next)
    for t, (g, head) in enumerate(tasks):
        m, m_next = m_next, [None]
        denom, acc = [None], [None]
        if t + 1 < len(tasks):
            qm = masked_queries(t + 1)
        waiting = []
        for c in range(n_blk):
            if t + 1 < len(tasks):
                score_block(t + 1, c, qm, m_next)
            waiting.append((c, exp_block(t, c, m, denom)))
            if len(waiting) > PV_LAG:
                pv_block(t, *waiting.pop(0), acc)
        for item in waiting:
            pv_block(t, *item, acc)
        groups[g][4](head // 2, head_rows(head), acc[0] * (1.0 / denom[0]))


NA_QBLKS_PER_STEP = 4


def _na_lat_kernel(*refs):
    n = NA_QBLKS_PER_STEP
    q_ref, kc_ref = refs[0], refs[1]
    k_refs = refs[2:2 + n * NA_LOC_BLKS]
    vc_ref = refs[2 + n * NA_LOC_BLKS]
    v_refs = refs[3 + n * NA_LOC_BLKS:3 + 2 * n * NA_LOC_BLKS]
    bias_refs = refs[3 + 2 * n * NA_LOC_BLKS:3 + 2 * n * NA_LOC_BLKS + n]
    o_ref, s_ref = refs[-2], refs[-1]

    def group(a):
        cols = slice(a * NA_QBLK, (a + 1) * NA_QBLK)

        def store(pair, rows, value):
            o_ref[0, pair, rows, cols] = value

        return (lambda pair: q_ref[0, pair, cols, :],
                (kc_ref,) + tuple(k_refs[a * NA_LOC_BLKS:(a + 1) * NA_LOC_BLKS]),
                (vc_ref,) + tuple(v_refs[a * NA_LOC_BLKS:(a + 1) * NA_LOC_BLKS]),
                lambda head, c: bias_refs[a][0, head, 0, c * KEY_BLK:(c + 1) * KEY_BLK, :],
                store)

    _na_heads([group(a) for a in range(n)], s_ref)


def _na_lat(qn, kn_c, kn, vnt_c, vnt, bias, layer):
    b, _, t, _ = qn.shape
    n = NA_QBLKS_PER_STEP
    n_qblk = t // NA_QBLK
    last_start = t // KEY_BLK - NA_LOC_BLKS

    def start(j):
        return jnp.clip(j - 1, 0, last_start)

    def pattern(j):
        return jnp.where(j == 0, 0, jnp.where(j == n_qblk - 1, 2, 1))

    row_blk = (1, NA_PAIRS, KEY_BLK, LANES)
    col_blk = (1, NA_PAIRS, LANES, KEY_BLK)
    k_specs = [pl.BlockSpec(row_blk, functools.partial(lambda bi, j, a, i: (bi, 0, start(n * j + a) + i, 0), a=a, i=i))
               for a in range(n) for i in range(NA_LOC_BLKS)]
    v_specs = [pl.BlockSpec(col_blk, functools.partial(lambda bi, j, a, i: (bi, 0, 0, start(n * j + a) + i), a=a, i=i))
               for a in range(n) for i in range(NA_LOC_BLKS)]
    bias_specs = [pl.BlockSpec((1, NA_HEADS, 1, NA_LOC_BLKS * KEY_BLK, NA_QBLK),
                               functools.partial(lambda bi, j, a: (layer, 0, pattern(n * j + a), 0, 0), a=a))
                  for a in range(n)]
    return pl.pallas_call(
        _na_lat_kernel,
        grid=(b, n_qblk // n),
        in_specs=[pl.BlockSpec((1, NA_PAIRS, n * NA_QBLK, LANES), lambda bi, j: (bi, 0, j, 0)),
                  pl.BlockSpec(row_blk, lambda bi, j: (bi, 0, 0, 0))] + k_specs
                 + [pl.BlockSpec(col_blk, lambda bi, j: (bi, 0, 0, 0))] + v_specs + bias_specs,
        out_specs=pl.BlockSpec((1, NA_PAIRS, LANES, n * NA_QBLK), lambda bi, j: (bi, 0, 0, j)),
        out_shape=jax.ShapeDtypeStruct((b, NA_PAIRS, LANES, t), F32),
        scratch_shapes=[pltpu.VMEM((2, (1 + NA_LOC_BLKS) * KEY_BLK, NA_QBLK), F32)],
        compiler_params=_cparams(2),
        name="na_lat",
    )(qn, kn_c, *([kn] * (n * NA_LOC_BLKS)), vnt_c, *([vnt] * (n * NA_LOC_BLKS)), *([bias] * n))


DIFF_SUB = 256
DIFF_KEY_BLK = 256


def _diff_body(q_ref, block_rows, key_blocks, val_blocks, lam_ref, g_ref, o_ref, s_ref, lam_init):
    n_sub = q_ref.shape[2] // DIFF_SUB
    tasks = [(h, i) for h in range(q_ref.shape[1]) for i in range(n_sub)]
    edges = np.cumsum([0] + list(block_rows))
    rows = [slice(int(a), int(b)) for a, b in zip(edges[:-1], edges[1:])]
    lp = lam_ref[...]
    lam = (jnp.exp(jnp.sum(lp[0:1] * lp[1:2], axis=1, keepdims=True))
           - jnp.exp(jnp.sum(lp[2:3] * lp[3:4], axis=1, keepdims=True)) + lam_init)
    gain = g_ref[...] * (1.0 - lam_init)

    def masked_queries(t):
        h, i = tasks[t]
        q = q_ref[0, h, i * DIFF_SUB:(i + 1) * DIFF_SUB, :]
        lane = lax.broadcasted_iota(jnp.int32, q.shape, 1)
        zero = jnp.zeros_like(q)
        return (jnp.where(lane < DIFF_QK_DIM, q, zero),
                jnp.where((lane >= DIFF_QK_DIM) & (lane < 2 * DIFF_QK_DIM), q, zero))

    def score_block(t, c, qm, m):
        k = key_blocks[c](tasks[t][0])
        for mp in range(2):
            s = _dot_nt(k, qm[mp])
            s_ref[t % 2, mp, rows[c], :] = s
            mc = jnp.max(s, axis=0, keepdims=True)
            m[mp] = mc if m[mp] is None else jnp.maximum(m[mp], mc)

    def exp_block(t, c, m, l):
        ps = []
        for mp in range(2):
            p = jnp.exp2(s_ref[t % 2, mp, rows[c], :] - m[mp])
            lc = jnp.sum(p, axis=0, keepdims=True)
            l[mp] = lc if l[mp] is None else l[mp] + lc
            ps.append(p.astype(BF16))
        return ps

    def pv_block(t, c, ps, acc):
        vt = val_blocks[c](tasks[t][0])
        for mp in range(2):
            pv = _dot(vt, ps[mp])
            acc[mp] = pv if acc[mp] is None else acc[mp] + pv

    n_blk = len(key_blocks)
    m_next = [None, None]
    qm = masked_queries(0)
    for c in range(n_blk):
        score_block(0, c, qm, m_next)
    for t, (h, i) in enumerate(tasks):
        m, m_next = m_next, [None, None]
        l, acc = [None, None], [None, None]
        if t + 1 < len(tasks):
            qm = masked_queries(t + 1)
        waiting = []
        for c in range(n_blk):
            if t + 1 < len(tasks):
                score_block(t + 1, c, qm, m_next)
            waiting.append((c, exp_block(t, c, m, l)))
            if len(waiting) > PV_LAG:
                pv_block(t, *waiting.pop(0), acc)
        for item in waiting:
            pv_block(t, *item, acc)
        out = acc[0] * (1.0 / l[0]) - acc[1] * (lam / l[1])
        ms = jnp.mean(out * out, axis=0, keepdims=True)
        o_ref[0, h, :, i * DIFF_SUB:(i + 1) * DIFF_SUB] = out * lax.rsqrt(ms + EPS) * gain


def _diff_lat_kernel(q_ref, kc_ref, k_ref, vc_ref, v_ref, lam_ref, g_ref, o_ref, s_ref, *, lam_init):
    n_lat = k_ref.shape[2] // DIFF_KEY_BLK
    blk = DIFF_KEY_BLK
    keys = [lambda h: kc_ref[0, h]] + [
        functools.partial(lambda c, h: k_ref[0, h, c * blk:(c + 1) * blk, :], c) for c in range(n_lat)]
    vals = [lambda h: vc_ref[0, h]] + [
        functools.partial(lambda c, h: v_ref[0, h, :, c * blk:(c + 1) * blk], c) for c in range(n_lat)]
    _diff_body(q_ref, [kc_ref.shape[2]] + [blk] * n_lat, keys, vals, lam_ref, g_ref, o_ref, s_ref, lam_init)


def _diff_ctx_kernel(q_ref, kc_ref, vc_ref, lam_ref, g_ref, o_ref, s_ref, *, lam_init):
    _diff_body(q_ref, [kc_ref.shape[2]], [lambda h: kc_ref[0, h]], [lambda h: vc_ref[0, h]], lam_ref, g_ref,
               o_ref, s_ref, lam_init)


def _diff_scratch(n_keys):
    return [pltpu.VMEM((2, 2, n_keys, DIFF_SUB), F32)]


DIFF_HEADS_PER_STEP = 2


def _diff_lat(qd, kd_c, kd, vdt_c, vdt, lam_p, g_col, lam_init, qb):
    b, _, t, _ = qd.shape
    c = kd_c.shape[2]
    heads = lambda shape, idx: pl.BlockSpec((1, DIFF_HEADS_PER_STEP) + shape, idx)
    return pl.pallas_call(
        functools.partial(_diff_lat_kernel, lam_init=lam_init),
        grid=(b, DIFF_HEADS // DIFF_HEADS_PER_STEP, t // qb),
        in_specs=[heads((qb, LANES), lambda bi, h, j: (bi, h, j, 0)),
                  heads((c, LANES), lambda bi, h, j: (bi, h, 0, 0)),
                  heads((t, LANES), lambda bi, h, j: (bi, h, 0, 0)),
                  heads((DIFF_V_DIM, c), lambda bi, h, j: (bi, h, 0, 0)),
                  heads((DIFF_V_DIM, t), lambda bi, h, j: (bi, h, 0, 0)),
                  pl.BlockSpec(lam_p.shape, lambda bi, h, j: (0, 0)),
                  pl.BlockSpec(g_col.shape, lambda bi, h, j: (0, 0))],
        out_specs=heads((DIFF_V_DIM, qb), lambda bi, h, j: (bi, h, 0, j)),
        out_shape=jax.ShapeDtypeStruct((b, DIFF_HEADS, DIFF_V_DIM, t), F32),
        scratch_shapes=_diff_scratch(c + t),
        compiler_params=_cparams(3),
        name="diff_lat",
    )(qd, kd_c, kd, vdt_c, vdt, lam_p, g_col)


CONV_PAD = 16
CONV_TILE = 256
SUBLANES = 8


def _conv_kernel(u_ref, w_ref, b_ref, g_ref, beta_ref, o_ref, sh_ref):
    s = u_ref.shape[1]
    zeros = jnp.zeros((CONV_PAD, CONV_WIDTH), F32)
    sh_ref[0, 0:CONV_PAD, :] = zeros
    sh_ref[0, CONV_PAD + s:CONV_PAD + s + CONV_PAD, :] = zeros
    sh_ref[0, CONV_PAD:CONV_PAD + s, :] = u_ref[0]
    first = CONV_PAD - CONV_TAPS // 2
    n_rows = s + 2 * CONV_PAD - SUBLANES
    for r in range(1, SUBLANES):
        sh_ref[r, 0:n_rows, :] = sh_ref[0, r:r + n_rows, :]
    for t0 in range(0, s, CONV_TILE):
        acc = jnp.broadcast_to(b_ref[...], (CONV_TILE, CONV_WIDTH))
        for j in range(CONV_TAPS):
            r = (first + j) % SUBLANES
            base = t0 + first + j - r
            acc = acc + sh_ref[r, base:base + CONV_TILE, :] * w_ref[j:j + 1, :]
        mu = jnp.mean(acc, axis=-1, keepdims=True)
        d = acc - mu
        var = jnp.mean(d * d, axis=-1, keepdims=True)
        y = d * lax.rsqrt(var + EPS) * g_ref[...] + beta_ref[...]
        o_ref[0, t0:t0 + CONV_TILE, :] = _silu(y)


def _conv(u, conv_w, conv_b, ln_g, ln_b):
    b, s, _ = u.shape
    row = lambda a: a.reshape(1, CONV_WIDTH)
    vec_spec = pl.BlockSpec((1, CONV_WIDTH), lambda bi: (0, 0))
    return pl.pallas_call(
        _conv_kernel,
        grid=(b,),
        in_specs=[pl.BlockSpec((1, s, CONV_WIDTH), lambda bi: (bi, 0, 0)),
                  pl.BlockSpec((CONV_TAPS, CONV_WIDTH), lambda bi: (0, 0)),
                  vec_spec, vec_spec, vec_spec],
        out_specs=pl.BlockSpec((1, s, CONV_WIDTH), lambda bi: (bi, 0, 0)),
        out_shape=jax.ShapeDtypeStruct((b, s, CONV_WIDTH), F32),
        scratch_shapes=[pltpu.VMEM((SUBLANES, s + 2 * CONV_PAD, CONV_WIDTH), F32)],
        compiler_params=_cparams(1),
        name="conformer_conv",
    )(u, conv_w, row(conv_b), row(ln_g), row(ln_b))


def _outproj_kernel(ont_ref, odt_ref, cv_ref, gate_ref, x_ref, w_ref, gpost_ref, gmod_ref, o_ref):
    tm = x_ref.shape[1]
    a = ont_ref[0].reshape(NA_WIDTH, tm).T
    bq = odt_ref[0].reshape(DIFF_WIDTH, tm).T
    mixed = (jnp.concatenate([a, bq, cv_ref[0]], axis=1) * gate_ref[0]).astype(BF16)
    y = _dot(mixed, w_ref[0])
    ms = jnp.mean(y * y, axis=-1, keepdims=True)
    o_ref[0] = x_ref[0] + gmod_ref[0] * (y * lax.rsqrt(ms + EPS) * gpost_ref[...])


def _outproj_io(ont, odt, cv, gate, x, w, layer, g_post, gmod, tm):
    b, s, _ = x.shape
    gm_idx = (lambda bi, i: (bi, 0, 0)) if gmod.shape[0] == b else (lambda bi, i: (0, 0, 0))
    rows = lambda width: pl.BlockSpec((1, tm, width), lambda bi, i: (bi, i, 0))
    in_specs = [pl.BlockSpec((1, NA_PAIRS, LANES, tm), lambda bi, i: (bi, 0, 0, i)),
                pl.BlockSpec((1, DIFF_HEADS, DIFF_V_DIM, tm), lambda bi, i: (bi, 0, 0, i)),
                rows(CONV_WIDTH), rows(D_MODEL), rows(D_MODEL),
                pl.BlockSpec((1, D_MODEL, D_MODEL), lambda bi, i: (layer, 0, 0)),
                pl.BlockSpec((1, D_MODEL), lambda bi, i: (0, 0)),
                pl.BlockSpec((1, 1, D_MODEL), gm_idx)]
    args = [ont, odt, cv, gate, x, w, g_post.reshape(1, D_MODEL), gmod]
    return in_specs, args, rows(D_MODEL), jax.ShapeDtypeStruct((b, s, D_MODEL), F32)


def _outproj(ont, odt, cv, gate, x, w, layer, g_post, gmod, tm):
    in_specs, args, out_spec, out_shape = _outproj_io(ont, odt, cv, gate, x, w, layer, g_post, gmod, tm)
    return pl.pallas_call(
        _outproj_kernel,
        grid=(x.shape[0], x.shape[1] // tm),
        in_specs=in_specs,
        out_specs=out_spec,
        out_shape=out_shape,
        compiler_params=_cparams(2),
        name="outproj",
    )(*args)


def _outin_kernel(*refs, rope):
    n_out_in = 8
    n_in_in = 5 if rope else 3
    out_in = refs[:n_out_in]
    in_in = refs[n_out_in:n_out_in + n_in_in]
    x_new_ref = refs[n_out_in + n_in_in]
    in_out = refs[n_out_in + n_in_in + 1:]
    _outproj_kernel(*out_in, x_new_ref)
    _inproj_kernel(x_new_ref, *in_in, *in_out, rope=rope)


def _outproj_inproj(out_args, in_args, tm):
    x = out_args[4]
    b, s, _ = x.shape
    o_in_specs, o_args, o_out_spec, o_out_shape = _outproj_io(*out_args, tm)
    i_in_specs, i_args, i_out_specs, i_out_shape = _inproj_io(b, s, *in_args, tm)
    rope = in_args[-1] is not None
    return pl.pallas_call(
        functools.partial(_outin_kernel, rope=rope),
        grid=(b, s // tm),
        in_specs=o_in_specs + i_in_specs,
        out_specs=[o_out_spec] + i_out_specs,
        out_shape=[o_out_shape] + i_out_shape,
        compiler_params=_cparams(2),
        name="outin_lat" if rope else "outin_ctx",
    )(*o_args, *i_args)


def _ctx_layer_kernel(qn_ref, kn_ref, vnt_ref, qd_ref, kd_ref, vdt_ref, lam_ref, g_ref,
                      u_ref, cw_ref, cb_ref, cg_ref, cbeta_ref, gate_ref, x_ref, wout_ref, gpost_ref, gmod_ref,
                      ss_ref, gpre_ref, win_ref, xnew_ref, *rest, lam_init):
    in_out = rest[:8]
    na_s, diff_s, sh_ref, ont_s, odt_s, cv_s = rest[8:]

    def store(pair, rows, value):
        ont_s[0, pair, rows, :] = value

    _na_heads([(lambda pair: qn_ref[0, pair], (kn_ref,), (vnt_ref,), None, store)], na_s)
    for h in range(DIFF_HEADS):
        head = lambda ref: ref.at[:, h:h + 1]
        _diff_ctx_kernel(head(qd_ref), head(kd_ref), head(vdt_ref), lam_ref, g_ref, head(odt_s), diff_s.at[h],
                         lam_init=lam_init)
    _conv_kernel(u_ref, cw_ref, cb_ref, cg_ref, cbeta_ref, cv_s, sh_ref)
    _outproj_kernel(ont_s, odt_s, cv_s, gate_ref, x_ref, wout_ref, gpost_ref, gmod_ref, xnew_ref)
    _inproj_kernel(xnew_ref, ss_ref, gpre_ref, win_ref, *in_out, rope=False)


def _ctx_layer(cx, xc, lam_p, g_col, lam_init, conv_p, w_out, layer, g_post, gmod, in_args):
    kn, kd, qn, qd, vnt, vdt, gate, u = cx
    b, c, _ = xc.shape
    whole = lambda a: pl.BlockSpec((1,) + a.shape[1:], lambda bi, i: (bi,) + (0,) * (a.ndim - 1))
    const = lambda a: pl.BlockSpec(a.shape, lambda bi, i: (0,) * a.ndim)
    conv_w, conv_b, ln_g, ln_b = conv_p
    row = lambda v: v.reshape(1, -1)
    small = [lam_p, g_col]
    conv_args = [conv_w, row(conv_b), row(ln_g), row(ln_b)]
    i_in_specs, i_args, i_out_specs, i_out_shape = _inproj_io(b, c, *in_args, c)
    args = ([qn, kn, vnt, qd, kd, vdt] + small + [u] + conv_args + [gate, xc, w_out, row(g_post), gmod] + i_args)
    in_specs = ([whole(a) for a in (qn, kn, vnt, qd, kd, vdt)] + [const(a) for a in small] + [whole(u)]
                + [const(a) for a in conv_args] + [whole(gate), whole(xc),
                                                   pl.BlockSpec((1, D_MODEL, D_MODEL), lambda bi, i: (layer, 0, 0)),
                                                   const(row(g_post)), const(gmod)] + i_in_specs)
    scratch = [pltpu.VMEM((2, c, c), F32),
               pltpu.VMEM((DIFF_HEADS, 2, 2, c, DIFF_SUB), F32),
               pltpu.VMEM((SUBLANES, c + 2 * CONV_PAD, CONV_WIDTH), F32),
               pltpu.VMEM((1, NA_PAIRS, LANES, c), F32),
               pltpu.VMEM((1, DIFF_HEADS, DIFF_V_DIM, c), F32),
               pltpu.VMEM((1, c, CONV_WIDTH), F32)]
    return pl.pallas_call(
        functools.partial(_ctx_layer_kernel, lam_init=lam_init),
        grid=(b, 1),
        in_specs=in_specs,
        out_specs=[whole(xc)] + i_out_specs,
        out_shape=[jax.ShapeDtypeStruct(xc.shape, F32)] + i_out_shape,
        scratch_shapes=scratch,
        compiler_params=_cparams(2),
        name="ctx_layer",
    )(*args)


PROJ_TM = 512
DIFF_QB = 2048


def kernel(x, c, ctx, c_ctx, w_mod, b_mod, g_pre, g_post, w_in, w_out, na_rpb, diff_lq1, diff_lk1,
           diff_lq2, diff_lk2, diff_subln_g, conv_w, conv_b, conv_ln_g, conv_ln_b):
    batch, seq, _ = x.shape
    pad_rows = (-(batch + 1)) % 8
    cc = jnp.concatenate([c, c_ctx[None, :], jnp.zeros((pad_rows, D_MODEL), F32)], axis=0)
    mod = _modulation(cc, w_mod, b_mod)
    bias = _na_bias_tables(na_rpb.reshape(-1))
    w_in_b = w_in.astype(BF16)
    w_out_b = w_out.astype(BF16)
    tables = _rope_tables(seq)
    lam_p = jnp.stack([diff_lq1, diff_lk1, diff_lq2, diff_lk2], axis=1)

    xc = ctx
    ctx_tm = xc.shape[1]

    def in_args(l, latent):
        ss = mod[l, :batch, :2 * D_MODEL].reshape(batch, 1, 1, 2 * D_MODEL) if latent else \
            mod[l, batch, :2 * D_MODEL].reshape(1, 1, 1, 2 * D_MODEL)
        return ss, g_pre[l], w_in_b, l, tables if latent else None

    lat = _inproj(x, *in_args(0, True), PROJ_TM)
    cx = _inproj(xc, *in_args(0, False), ctx_tm)
    for l in range(DEPTH):
        last = l == DEPTH - 1
        lam_init = 0.8 - 0.6 * math.exp(-0.3 * l)
        g_col = diff_subln_g[l].reshape(DIFF_V_DIM, 1)
        kn, kd, qn, qd, vnt, vdt, gate, u = lat
        kn_c, kd_c, qn_c, qd_c, vnt_c, vdt_c, gate_c, u_c = cx

        ont = _na_lat(qn, kn_c, kn, vnt_c, vnt, bias, l)
        odt = _diff_lat(qd, kd_c, kd, vdt_c, vdt, lam_p[l], g_col, lam_init, DIFF_QB)
        cv = _conv(u, conv_w[l], conv_b[l], conv_ln_g[l], conv_ln_b[l])
        gmod = mod[l, :batch, 2 * D_MODEL:].reshape(batch, 1, D_MODEL)
        out_args = (ont, odt, cv, gate, x, w_out_b, l, g_post[l], gmod)
        if last:
            return _outproj(*out_args, PROJ_TM)

        gmod_c = mod[l, batch, 2 * D_MODEL:].reshape(1, 1, D_MODEL)
        conv_p = (conv_w[l], conv_b[l], conv_ln_g[l], conv_ln_b[l])
        xc, *cx_next = _ctx_layer(cx, xc, lam_p[l], g_col, lam_init, conv_p, w_out_b, l, g_post[l], gmod_c,
                                  in_args(l + 1, False))
        x, *lat = _outproj_inproj(out_args, in_args(l + 1, True), PROJ_TM)
        cx = cx_next
```
